```python
import math
import jax, jax.numpy as jnp
from jax import lax
import numpy as np

D_MODEL = 2048
BATCH = 4
SEQ = 4096
DEPTH = 2

GRID_W = 64
CTX_LEN = 256
Q_BLOCK = 128
ROPE_BASE = 10000.0
EPS = 1e-6

MLA_HEADS = 8
MLA_Q_LORA = 512
MLA_KV_LORA = 512
MLA_NOPE = 128
MLA_ROPE = 64
MLA_V = 128

DIFF_HEADS = 4
DIFF_DIM = 128
DIFF_QK = 2 * DIFF_HEADS * DIFF_DIM

EVEN_SPLITS = (MLA_Q_LORA,
               MLA_Q_LORA + MLA_KV_LORA,
               MLA_Q_LORA + MLA_KV_LORA + MLA_ROPE,
               MLA_Q_LORA + MLA_KV_LORA + MLA_ROPE + DIFF_QK,
               MLA_Q_LORA + MLA_KV_LORA + MLA_ROPE + 2 * DIFF_QK)
EVEN_IN = MLA_Q_LORA + MLA_KV_LORA + MLA_ROPE + 3 * DIFF_QK
EVEN_OUT = MLA_HEADS * MLA_V + DIFF_HEADS * 2 * DIFF_DIM

POOL_WINDOWS = (2, 4, 8, 16)
POOL_WIDTH = D_MODEL // 2
POOL_GROUP = POOL_WIDTH // 4
FOURIER_GROUPS = 4
FOURIER_WIDTH = D_MODEL - POOL_WIDTH
FOURIER_GROUP = FOURIER_WIDTH // FOURIER_GROUPS
ODD_WIDTH = POOL_WIDTH + FOURIER_WIDTH

N_EXPERTS = 16
EC_CAPACITY = 2
EXPERT_FF = D_MODEL

kernel_name = 'hybrid_mla_diffattn_pool_fnet_ecmoe_dit'


def rms_norm(x, g):
    xf = x.astype(jnp.float32)
    y = xf * lax.rsqrt(jnp.mean(xf * xf, axis=-1, keepdims=True) + EPS)
    return (y * g.astype(jnp.float32)).astype(x.dtype)


def rope_2d_tables(n, rot_dim):
    rows = n // GRID_W
    row = jnp.repeat(jnp.arange(rows, dtype=jnp.float32), GRID_W)
    col = jnp.tile(jnp.arange(GRID_W, dtype=jnp.float32), rows)
    quarter = rot_dim // 4
    inv_freq = ROPE_BASE ** (-jnp.arange(quarter, dtype=jnp.float32) / quarter)
    ang = jnp.stack([row[:, None] * inv_freq, col[:, None] * inv_freq], axis=1)
    return jnp.cos(ang)[:, None], jnp.sin(ang)[:, None]


def apply_rope_2d(x, cos, sin):
    shape = x.shape
    r = shape[-1]
    xr = x.reshape(shape[0], shape[1], -1, 2, 2, r // 4).astype(jnp.float32)
    x1, x2 = xr[..., 0, :], xr[..., 1, :]
    out = jnp.stack([x1 * cos - x2 * sin, x2 * cos + x1 * sin], axis=-2)
    return out.reshape(shape).astype(x.dtype)


def block_attention(q, k, v, scale):
    b, nq, m, h, dk = q.shape
    nb = nq // Q_BLOCK
    qb = jnp.moveaxis(q.reshape(b, nb, Q_BLOCK, m, h, dk), 1, 0)

    def one_block(qi):
        s = jnp.einsum('bqmhd,bkmhd->bmhqk', qi, k, preferred_element_type=jnp.float32) * scale
        p = jax.nn.softmax(s, axis=-1).astype(v.dtype)
        return jnp.einsum('bmhqk,bkhd->bqmhd', p, v)

    o = lax.map(one_block, qb)
    return jnp.moveaxis(o, 0, 1).reshape(b, nq, m, h, v.shape[-1])


def even_project(h, w_in, q_norm, w_uq, kv_norm, w_ukv, rope_a, rope_b, need_q):
    b, n, _ = h.shape
    cq, ckv, kr, qd, kd, vd = jnp.split(h @ w_in, EVEN_SPLITS, axis=-1)
    kv = (rms_norm(ckv, kv_norm) @ w_ukv).reshape(b, n, MLA_HEADS, MLA_NOPE + MLA_V)
    k_nope, v_a = kv[..., :MLA_NOPE], kv[..., MLA_NOPE:]
    k_rope = kr.reshape(b, n, 1, MLA_ROPE)
    kd = kd.reshape(b, n, 2, DIFF_HEADS, DIFF_DIM)
    vd = vd.reshape(b, n, DIFF_HEADS, 2 * DIFF_DIM)
    if rope_a is not None:
        k_rope = apply_rope_2d(k_rope, *rope_a)
        kd = apply_rope_2d(kd, *rope_b)
    k_a = jnp.concatenate([k_nope, jnp.broadcast_to(k_rope, (b, n, MLA_HEADS, MLA_ROPE))], axis=-1)[:, :, None]
    if not need_q:
        return None, k_a, v_a, None, kd, vd
    q = (rms_norm(cq, q_norm) @ w_uq).reshape(b, n, MLA_HEADS, MLA_NOPE + MLA_ROPE)
    q_nope, q_rope = q[..., :MLA_NOPE], q[..., MLA_NOPE:]
    qd = qd.reshape(b, n, 2, DIFF_HEADS, DIFF_DIM)
    if rope_a is not None:
        q_rope = apply_rope_2d(q_rope, *rope_a)
        qd = apply_rope_2d(qd, *rope_b)
    q_a = jnp.concatenate([q_nope, q_rope], axis=-1)[:, :, None]
    return q_a, k_a, v_a, qd, kd, vd


def even_heads_out(o_a, o_d, lam, lam_init, subln, w_out):
    b, n = o_a.shape[:2]
    od = o_d[:, :, 0] - lam.astype(o_d.dtype) * o_d[:, :, 1]
    od = rms_norm(od, subln) * (1.0 - lam_init)
    y = jnp.concatenate([o_a.reshape(b, n, -1), od.reshape(b, n, -1)], axis=-1)
    return y @ w_out


def even_mixer(h_lat, h_ctx, layer_idx, w_in, q_norm, w_uq, kv_norm, w_ukv,
               lq1, lk1, lq2, lk2, subln, w_out, with_ctx_out):
    n = h_lat.shape[1]
    rope_a = rope_2d_tables(n, MLA_ROPE)
    rope_b = rope_2d_tables(n, DIFF_DIM)
    qa_l, ka_l, va_l, qd_l, kd_l, vd_l = even_project(h_lat, w_in, q_norm, w_uq, kv_norm, w_ukv, rope_a, rope_b, True)
    qa_c, ka_c, va_c, qd_c, kd_c, vd_c = even_project(h_ctx, w_in, q_norm, w_uq, kv_norm, w_ukv, None, None, with_ctx_out)
    lam_init = 0.8 - 0.6 * math.exp(-0.3 * layer_idx)
    lam = (jnp.exp(jnp.sum(lq1.astype(jnp.float32) * lk1.astype(jnp.float32)))
           - jnp.exp(jnp.sum(lq2.astype(jnp.float32) * lk2.astype(jnp.float32))) + lam_init)
    scale_a = 1.0 / math.sqrt(MLA_NOPE + MLA_ROPE)
    scale_d = 1.0 / math.sqrt(DIFF_DIM)
    oa = block_attention(qa_l, jnp.concatenate([ka_c, ka_l], axis=1), jnp.concatenate([va_c, va_l], axis=1), scale_a)[:, :, 0]
    od = block_attention(qd_l, jnp.concatenate([kd_c, kd_l], axis=1), jnp.concatenate([vd_c, vd_l], axis=1), scale_d)
    y_lat = even_heads_out(oa, od, lam, lam_init, subln, w_out)
    y_ctx = None
    if with_ctx_out:
        oa_c = block_attention(qa_c, ka_c, va_c, scale_a)[:, :, 0]
        od_c = block_attention(qd_c, kd_c, vd_c, scale_d)
        y_ctx = even_heads_out(oa_c, od_c, lam, lam_init, subln, w_out)
    return y_lat, y_ctx


def multi_scale_pool(z):
    b, n, g, cg = z.shape
    zf = z.astype(jnp.float32)
    csum = jnp.concatenate([jnp.zeros((b, 1, g, cg), jnp.float32), jnp.cumsum(zf, axis=1)], axis=1)
    t = jnp.arange(n)[:, None]
    half = jnp.array(POOL_WINDOWS, dtype=jnp.int32)[None, :] // 2
    lo = jnp.clip(t - half, 0, n)
    hi = jnp.clip(t + half, 0, n)
    gidx = jnp.arange(g)[None, :]
    window_sum = csum[:, hi, gidx] - csum[:, lo, gidx]
    mean = window_sum / (hi - lo).astype(jnp.float32)[None, :, :, None]
    return (mean - zf).astype(z.dtype)


def odd_mixer(h, w_in, pool_w, pool_scale, fourier_w, w_out):
    b, n, _ = h.shape
    z = h @ w_in
    zp = z[..., :POOL_WIDTH].reshape(b, n, len(POOL_WINDOWS), POOL_GROUP)
    zf = z[..., POOL_WIDTH:].reshape(b, n, FOURIER_GROUPS, FOURIER_GROUP)
    yp = jnp.einsum('bngc,gcd->bngd', multi_scale_pool(zp), pool_w).reshape(b, n, POOL_WIDTH) * pool_scale
    spec = jnp.fft.fft2(zf.astype(jnp.float32), axes=(1, 3), norm='ortho').real.astype(z.dtype)
    yf = jnp.einsum('bngc,gcd->bngd', spec, fourier_w).reshape(b, n, FOURIER_WIDTH)
    return jnp.concatenate([yp, yf], axis=-1) @ w_out


def expert_choice_ffn(h, w_router, w_gate, w_up, w_down):
    b, n, d = h.shape
    cap = (EC_CAPACITY * n) // N_EXPERTS
    logits = jnp.einsum('bnd,de->ben', h, w_router, preferred_element_type=jnp.float32)
    aff = jax.nn.softmax(logits, axis=1)
    gate, idx = lax.top_k(aff, cap)
    xe = jax.vmap(lambda hb, ib: hb[ib])(h, idx)
    hid = jax.nn.silu(jnp.einsum('becd,edf->becf', xe, w_gate)) * jnp.einsum('becd,edf->becf', xe, w_up)
    ye = jnp.einsum('becf,efd->becd', hid, w_down) * gate[..., None].astype(h.dtype)
    return jax.vmap(lambda yb, ib: jnp.zeros((n, d), yb.dtype).at[ib.reshape(-1)].add(yb.reshape(-1, d)))(ye, idx)


def setup_inputs(seed: int = 0) -> dict:
    key = jax.random.key(seed)
    ks = iter(jax.random.split(key, 40))
    n_even = (DEPTH + 1) // 2
    n_odd = DEPTH // 2

    def nrm(shape, scale):
        return jax.random.normal(next(ks), shape, jnp.float32) * scale

    def gain(shape):
        return 1.0 + nrm(shape, 0.02)

    return {
        'x': nrm((BATCH, SEQ, D_MODEL), 1.0),
        'c': nrm((BATCH, D_MODEL), 1.0),
        'ctx': nrm((BATCH, CTX_LEN, D_MODEL), 1.0),
        'c_ctx': nrm((D_MODEL,), 1.0),
        'mod_w': nrm((DEPTH, D_MODEL, 6 * D_MODEL), 0.5 * D_MODEL ** -0.5),
        'mod_b': nrm((DEPTH, 6 * D_MODEL), 0.02),
        'norm_mix': gain((DEPTH, D_MODEL)),
        'norm_ffn': gain((DEPTH, D_MODEL)),
        'even_w_in': nrm((n_even, D_MODEL, EVEN_IN), D_MODEL ** -0.5),
        'mla_q_norm': gain((n_even, MLA_Q_LORA)),
        'mla_w_uq': nrm((n_even, MLA_Q_LORA, MLA_HEADS * (MLA_NOPE + MLA_ROPE)), MLA_Q_LORA ** -0.5),
        'mla_kv_norm': gain((n_even, MLA_KV_LORA)),
        'mla_w_ukv': nrm((n_even, MLA_KV_LORA, MLA_HEADS * (MLA_NOPE + MLA_V)), MLA_KV_LORA ** -0.5),
        'diff_lambda_q1': nrm((n_even, DIFF_DIM), 0.1),
        'diff_lambda_k1': nrm((n_even, DIFF_DIM), 0.1),
        'diff_lambda_q2': nrm((n_even, DIFF_DIM), 0.1),
        'diff_lambda_k2': nrm((n_even, DIFF_DIM), 0.1),
        'diff_subln': gain((n_even, 2 * DIFF_DIM)),
        'even_w_out': nrm((n_even, EVEN_OUT, D_MODEL), EVEN_OUT ** -0.5),
        'odd_w_in': nrm((n_odd, D_MODEL, ODD_WIDTH), D_MODEL ** -0.5),
        'pool_w': nrm((n_odd, len(POOL_WINDOWS), POOL_GROUP, POOL_GROUP), POOL_GROUP ** -0.5),
        'pool_scale': gain((n_odd, POOL_WIDTH)),
        'fourier_w': nrm((n_odd, FOURIER_GROUPS, FOURIER_GROUP, FOURIER_GROUP), FOURIER_GROUP ** -0.5),
        'odd_w_out': nrm((n_odd, ODD_WIDTH, D_MODEL), ODD_WIDTH ** -0.5),
        'router_w': nrm((DEPTH, D_MODEL, N_EXPERTS), D_MODEL ** -0.5),
        'expert_w_gate': nrm((DEPTH, N_EXPERTS, D_MODEL, EXPERT_FF), D_MODEL ** -0.5),
        'expert_w_up': nrm((DEPTH, N_EXPERTS, D_MODEL, EXPERT_FF), D_MODEL ** -0.5),
        'expert_w_down': nrm((DEPTH, N_EXPERTS, EXPERT_FF, D_MODEL), EXPERT_FF ** -0.5),
        'final_norm': gain((D_MODEL,)),
    }


def reference(x, c, ctx, c_ctx, mod_w, mod_b, norm_mix, norm_ffn,
              even_w_in, mla_q_norm, mla_w_uq, mla_kv_norm, mla_w_ukv,
              diff_lambda_q1, diff_lambda_k1, diff_lambda_q2, diff_lambda_k2, diff_subln, even_w_out,
              odd_w_in, pool_w, pool_scale, fourier_w, odd_w_out,
              router_w, expert_w_gate, expert_w_up, expert_w_down, final_norm):
    xl, xc = x, ctx
    for i in range(DEPTH):
        j = i // 2
        ctx_live = any(k % 2 == 0 for k in range(i + 1, DEPTH))
        ctx_in = (i % 2 == 0) or ctx_live
        ml = (jax.nn.silu(c) @ mod_w[i] + mod_b[i])[:, None, :]
        sh1, sc1, g1, sh2, sc2, g2 = jnp.split(ml, 6, axis=-1)
        hl = rms_norm(xl, norm_mix[i]) * (1.0 + sc1) + sh1
        if ctx_in:
            mc = jax.nn.silu(c_ctx) @ mod_w[i] + mod_b[i]
            csh1, csc1, cg1, csh2, csc2, cg2 = jnp.split(mc, 6, axis=-1)
            hc = rms_norm(xc, norm_mix[i]) * (1.0 + csc1) + csh1
        if i % 2 == 0:
            yl, yc = even_mixer(hl, hc, i, even_w_in[j], mla_q_norm[j], mla_w_uq[j], mla_kv_norm[j], mla_w_ukv[j],
                                diff_lambda_q1[j], diff_lambda_k1[j], diff_lambda_q2[j], diff_lambda_k2[j],
                                diff_subln[j], even_w_out[j], ctx_live)
        else:
            yl = odd_mixer(hl, odd_w_in[j], pool_w[j], pool_scale[j], fourier_w[j], odd_w_out[j])
            yc = odd_mixer(hc, odd_w_in[j], pool_w[j], pool_scale[j], fourier_w[j], odd_w_out[j]) if ctx_live else None
        xl = xl + g1 * yl
        hl = rms_norm(xl, norm_ffn[i]) * (1.0 + sc2) + sh2
        xl = xl + g2 * expert_choice_ffn(hl, router_w[i], expert_w_gate[i], expert_w_up[i], expert_w_down[i])
        if ctx_live:
            xc = xc + cg1 * yc
            hc = rms_norm(xc, norm_ffn[i]) * (1.0 + csc2) + csh2
            xc = xc + cg2 * expert_choice_ffn(hc, router_w[i], expert_w_gate[i], expert_w_up[i], expert_w_down[i])
    return rms_norm(xl, final_norm)
```

```python
import functools
import math

import jax
import jax.numpy as jnp
from jax import lax
from jax.experimental import pallas as pl
from jax.experimental.pallas import tpu as pltpu

F32 = jnp.float32
BF = jnp.bfloat16
I32 = jnp.int32
U32 = jnp.uint32

GRID_W = 64
ROPE_BASE = 10000.0
EPS = 1e-6
MLA_HEADS = 8
MLA_Q_LORA = 512
MLA_KV_LORA = 512
MLA_NOPE = 128
MLA_ROPE = 64
MLA_V = 128
DIFF_HEADS = 4
DIFF_DIM = 128
DIFF_QK = 2 * DIFF_HEADS * DIFF_DIM
POOL_WINDOWS = (2, 4, 8, 16)
N_EXPERTS = 16
EC_CAPACITY = 2

LANE = 128
SUBLANE_BF16 = 16
VMEM_LIMIT = 56 * 1024 * 1024

TOKEN_BLOCK = 256
CHUNK_ROWS = 64
COUNT_CHUNK = 128
AFF_W = LANE


def _cparams(*sem):
    return pltpu.CompilerParams(dimension_semantics=sem, vmem_limit_bytes=VMEM_LIMIT)


def _tile(total, target, mult=SUBLANE_BF16):
    best = None
    for t in range(mult, min(total, target) + 1, mult):
        if total % t == 0:
            best = t
    assert best is not None, (total, target)
    return best


def _const_spec(shape):
    nd = len(shape)
    return pl.BlockSpec(shape, lambda *_: (0,) * nd, pipeline_mode=pl.Buffered(1))


def _mod_kernel(c_ref, w_ref, b_ref, o_ref):
    cv = c_ref[...]
    s = (cv * jax.nn.sigmoid(cv)).astype(BF)
    o_ref[0] = jnp.dot(s, w_ref[0].astype(BF), preferred_element_type=F32) + b_ref[0]


def _modulation(c8, mod_w, mod_b):
    depth, d, n6 = mod_w.shape
    tn = 1024
    return pl.pallas_call(
        _mod_kernel,
        grid=(depth, n6 // tn),
        in_specs=[
            pl.BlockSpec((8, d), lambda i, j: (0, 0)),
            pl.BlockSpec((1, d, tn), lambda i, j: (i, 0, j)),
            pl.BlockSpec((1, 1, tn), lambda i, j: (i, 0, j)),
        ],
        out_specs=pl.BlockSpec((1, 8, tn), lambda i, j: (i, 0, j)),
        out_shape=jax.ShapeDtypeStruct((depth, 8, n6), F32),
        compiler_params=_cparams("parallel", "parallel"),
        name="modulation",
    )(c8, mod_w, mod_b.reshape(depth, 1, n6))


def _norm_mod(xf, gain, scale, shift):
    ms = jnp.mean(xf * xf, axis=-1, keepdims=True)
    y = xf * lax.rsqrt(ms + EPS) * gain
    return y * (1.0 + scale) + shift


def _normmod_cat_kernel(x_ref, ctx_ref, g_ref, sc_ref, sh_ref, scc_ref, shc_ref, o_ref, *, n_lat_blocks):
    i = pl.program_id(1)

    @pl.when(i < n_lat_blocks)
    def _():
        o_ref[0] = _norm_mod(x_ref[0], g_ref[...], sc_ref[0], sh_ref[0]).astype(BF)

    @pl.when(i >= n_lat_blocks)
    def _():
        o_ref[0] = _norm_mod(ctx_ref[0], g_ref[...], scc_ref[...], shc_ref[...]).astype(BF)


def _normmod_cat(x, ctx, gain, sc, sh, scc, shc):
    b, n, d = x.shape
    n_ctx = ctx.shape[1]
    tm = math.gcd(n, n_ctx)
    tm = _tile(tm, 512)
    nl, nc = n // tm, n_ctx // tm
    return pl.pallas_call(
        functools.partial(_normmod_cat_kernel, n_lat_blocks=nl),
        grid=(b, nl + nc),
        in_specs=[
            pl.BlockSpec((1, tm, d), lambda bi, i: (bi, jnp.minimum(i, nl - 1), 0)),
            pl.BlockSpec((1, tm, d), lambda bi, i: (bi, jnp.maximum(i - nl, 0), 0)),
            pl.BlockSpec((1, d), lambda bi, i: (0, 0)),
            pl.BlockSpec((1, 1, d), lambda bi, i: (bi, 0, 0)),
            pl.BlockSpec((1, 1, d), lambda bi, i: (bi, 0, 0)),
            pl.BlockSpec((1, d), lambda bi, i: (0, 0)),
            pl.BlockSpec((1, d), lambda bi, i: (0, 0)),
        ],
        out_specs=pl.BlockSpec((1, tm, d), lambda bi, i: (bi, i, 0)),
        out_shape=jax.ShapeDtypeStruct((b, n + n_ctx, d), BF),
        compiler_params=_cparams("parallel", "parallel"),
        name="normmod_cat",
    )(x, ctx, gain, sc, sh, scc, shc)


def _normmod_kernel(x_ref, g_ref, sc_ref, sh_ref, o_ref):
    o_ref[0] = _norm_mod(x_ref[0], g_ref[...], sc_ref[0], sh_ref[0]).astype(BF)


def _normmod(x, gain, sc, sh):
    b, n, d = x.shape
    tm = _tile(n, 512)
    return pl.pallas_call(
        _normmod_kernel,
        grid=(b, n // tm),
        in_specs=[
            pl.BlockSpec((1, tm, d), lambda bi, i: (bi, i, 0)),
            pl.BlockSpec((1, d), lambda bi, i: (0, 0)),
            pl.BlockSpec((1, 1, d), lambda bi, i: (bi, 0, 0)),
            pl.BlockSpec((1, 1, d), lambda bi, i: (bi, 0, 0)),
        ],
        out_specs=pl.BlockSpec((1, tm, d), lambda bi, i: (bi, i, 0)),
        out_shape=jax.ShapeDtypeStruct((b, n, d), BF),
        compiler_params=_cparams("parallel", "parallel"),
        name="normmod",
    )(x, gain, sc, sh)


def _rope(xb, cs, sn):
    return xb * cs + pltpu.roll(xb, LANE // 2, 1) * sn


def _rope_tables(n, n_ctx, quarter, pad):
    t = jnp.arange(n)
    row = (t // GRID_W).astype(F32)
    col = (t % GRID_W).astype(F32)
    inv_freq = ROPE_BASE ** (-jnp.arange(quarter, dtype=F32) / quarter)
    ang = jnp.concatenate([row[:, None] * inv_freq, col[:, None] * inv_freq], axis=1)
    cos, sin = jnp.cos(ang), jnp.sin(ang)
    zpad = jnp.zeros((n, pad), F32)
    cs = jnp.concatenate([cos, zpad, cos, zpad], axis=1)
    sn = jnp.concatenate([-sin, zpad, sin, zpad], axis=1)
    cs = jnp.concatenate([cs, jnp.ones((n_ctx, LANE), F32)], axis=0)
    sn = jnp.concatenate([sn, jnp.zeros((n_ctx, LANE), F32)], axis=0)
    return cs, sn


def _rope_perm(quarter, pad):
    src = []
    for j in range(2):
        for a in range(2):
            for f in range(quarter):
                src.append(a * 2 * quarter + j * quarter + f)
        src.extend([-1] * pad)
    return src


def _take_cols(w, src):
    wz = jnp.concatenate([w, jnp.zeros((w.shape[0], 1), w.dtype)], axis=1)
    idx = jnp.array([s if s >= 0 else w.shape[1] for s in src], I32)
    return jnp.take(wz, idx, axis=1)


def _rms_gain(v, g):
    ms = jnp.mean(v * v, axis=-1, keepdims=True)
    return v * lax.rsqrt(ms + EPS) * g


def _proj_lora_kernel(a_ref, w_ref, gq_ref, gkv_ref, cs_ref, sn_ref, cq_ref, ckv_ref, kr_ref):
    acc = jnp.dot(a_ref[0], w_ref[...], preferred_element_type=F32)
    ql, kvl = MLA_Q_LORA, MLA_KV_LORA
    cq_ref[0] = _rms_gain(acc[:, :ql], gq_ref[...]).astype(BF)
    ckv_ref[0] = _rms_gain(acc[:, ql:ql + kvl], gkv_ref[...]).astype(BF)
    kr_ref[0] = _rope(acc[:, ql + kvl:], cs_ref[...], sn_ref[...]).astype(BF)


def _proj_lora(h, w, gq, gkv, cs, sn):
    b, nt, d = h.shape
    tm = _tile(nt, 1100)
    nw = w.shape[1]
    return pl.pallas_call(
        _proj_lora_kernel,
        grid=(b, nt // tm),
        in_specs=[
            pl.BlockSpec((1, tm, d), lambda bi, i: (bi, i, 0)),
            _const_spec((d, nw)),
            _const_spec((1, MLA_Q_LORA)),
            _const_spec((1, MLA_KV_LORA)),
            pl.BlockSpec((tm, LANE), lambda bi, i: (i, 0)),
            pl.BlockSpec((tm, LANE), lambda bi, i: (i, 0)),
        ],
        out_specs=[
            pl.BlockSpec((1, tm, MLA_Q_LORA), lambda bi, i: (bi, i, 0)),
            pl.BlockSpec((1, tm, MLA_KV_LORA), lambda bi, i: (bi, i, 0)),
            pl.BlockSpec((1, tm, LANE), lambda bi, i: (bi, i, 0)),
        ],
        out_shape=[
            jax.ShapeDtypeStruct((b, nt, MLA_Q_LORA), BF),
            jax.ShapeDtypeStruct((b, nt, MLA_KV_LORA), BF),
            jax.ShapeDtypeStruct((b, nt, LANE), BF),
        ],
        compiler_params=_cparams("parallel", "parallel"),
        name="proj_lora",
    )(h, w, gq, gkv, cs, sn)


def _proj_diff_kernel(a_ref, w_ref, cs_ref, sn_ref, o_ref, *, n_rope_tiles):
    j = pl.program_id(0)
    acc = jnp.dot(a_ref[0], w_ref[...], preferred_element_type=F32)
    tn = acc.shape[1]

    @pl.when(j < n_rope_tiles)
    def _():
        cs, sn = cs_ref[...], sn_ref[...]
        for blk in range(tn // LANE):
            sl = slice(blk * LANE, (blk + 1) * LANE)
            o_ref[0, :, sl] = _rope(acc[:, sl], cs, sn).astype(BF)

    @pl.when(j >= n_rope_tiles)
    def _():
        o_ref[0] = acc.astype(BF)


def _proj_diff(h, w, cs, sn):
    b, nt, d = h.shape
    tm = _tile(nt, 1100)
    tn = 1024
    nw = w.shape[1]
    return pl.pallas_call(
        functools.partial(_proj_diff_kernel, n_rope_tiles=2 * DIFF_QK // tn),
        grid=(nw // tn, b, nt // tm),
        in_specs=[
            pl.BlockSpec((1, tm, d), lambda j, bi, i: (bi, i, 0)),
            pl.BlockSpec((d, tn), lambda j, bi, i: (0, j)),
            pl.BlockSpec((tm, LANE), lambda j, bi, i: (i, 0)),
            pl.BlockSpec((tm, LANE), lambda j, bi, i: (i, 0)),
        ],
        out_specs=pl.BlockSpec((1, tm, tn), lambda j, bi, i: (bi, i, j)),
        out_shape=jax.ShapeDtypeStruct((b, nt, nw), BF),
        compiler_params=_cparams("parallel", "parallel", "parallel"),
        name="proj_diff",
    )(h, w, cs, sn)


def _up_q_kernel(a_ref, w_ref, cs_ref, sn_ref, o_ref):
    acc = jnp.dot(a_ref[0], w_ref[...], preferred_element_type=F32)
    cs, sn = cs_ref[...], sn_ref[...]
    for hd in range(MLA_HEADS):
        c0 = hd * 2 * LANE
        o_ref[0, :, c0:c0 + LANE] = acc[:, c0:c0 + LANE].astype(BF)
        o_ref[0, :, c0 + LANE:c0 + 2 * LANE] = _rope(acc[:, c0 + LANE:c0 + 2 * LANE], cs, sn).astype(BF)


def _up_q(cq, w, cs, sn, n):
    b, nt, k = cq.shape
    tm = _tile(n, 1024)
    nw = w.shape[1]
    return pl.pallas_call(
        _up_q_kernel,
        grid=(b, n // tm),
        in_specs=[
            pl.BlockSpec((1, tm, k), lambda bi, i: (bi, i, 0)),
            _const_spec((k, nw)),
            pl.BlockSpec((tm, LANE), lambda bi, i: (i, 0)),
            pl.BlockSpec((tm, LANE), lambda bi, i: (i, 0)),
        ],
        out_specs=pl.BlockSpec((1, tm, nw), lambda bi, i: (bi, i, 0)),
        out_shape=jax.ShapeDtypeStruct((b, n, nw), BF),
        compiler_params=_cparams("parallel", "parallel"),
        name="up_q",
    )(cq, w, cs, sn)


def _up_kv_kernel(a_ref, w_ref, kr_ref, k_ref, v_ref):
    acc = jnp.dot(a_ref[0], w_ref[...], preferred_element_type=F32)
    kr = kr_ref[0]
    for hd in range(MLA_HEADS):
        c0 = hd * 2 * LANE
        k_ref[0, :, c0:c0 + LANE] = acc[:, c0:c0 + LANE].astype(BF)
        k_ref[0, :, c0 + LANE:c0 + 2 * LANE] = kr
        v_ref[0, :, hd * LANE:(hd + 1) * LANE] = acc[:, c0 + LANE:c0 + 2 * LANE].astype(BF)


def _up_kv(ckv, w, kr):
    b, nt, k = ckv.shape
    tm = _tile(nt, 1100)
    nw = w.shape[1]
    return pl.pallas_call(
        _up_kv_kernel,
        grid=(b, nt // tm),
        in_specs=[
            pl.BlockSpec((1, tm, k), lambda bi, i: (bi, i, 0)),
            _const_spec((k, nw)),
            pl.BlockSpec((1, tm, LANE), lambda bi, i: (bi, i, 0)),
        ],
        out_specs=[
            pl.BlockSpec((1, tm, nw), lambda bi, i: (bi, i, 0)),
            pl.BlockSpec((1, tm, MLA_HEADS * MLA_V), lambda bi, i: (bi, i, 0)),
        ],
        out_shape=[
            jax.ShapeDtypeStruct((b, nt, nw), BF),
            jax.ShapeDtypeStruct((b, nt, MLA_HEADS * MLA_V), BF),
        ],
        compiler_params=_cparams("parallel", "parallel"),
        name="up_kv",
    )(ckv, w, kr)


def _softmax_attend(q, k_ref, v_ref, scale, kv_chunk):
    tq = q.shape[0]
    nk = k_ref.shape[1]
    dv = v_ref.shape[2]

    def step(c0, size, carry):
        m, l, acc = carry
        k = k_ref[0, pl.ds(c0, size), :]
        v = v_ref[0, pl.ds(c0, size), :]
        s = lax.dot_general(q, k, (((1,), (1,)), ((), ())), preferred_element_type=F32) * scale
        m_new = jnp.maximum(m, jnp.max(s, axis=-1, keepdims=True))
        alpha = jnp.exp(m - m_new)
        p = jnp.exp(s - m_new)
        l = alpha * l + jnp.sum(p, axis=-1, keepdims=True)
        acc = alpha * acc + jnp.dot(p.astype(BF), v, preferred_element_type=F32)
        return m_new, l, acc

    carry = (jnp.full((tq, 1), -jnp.inf, F32), jnp.zeros((tq, 1), F32), jnp.zeros((tq, dv), F32))
    n_full = nk // kv_chunk
    carry = lax.fori_loop(
        0, n_full, lambda c, cr: step(pl.multiple_of(c * kv_chunk, kv_chunk), kv_chunk, cr), carry)
    rem = nk - n_full * kv_chunk
    if rem:
        carry = step(n_full * kv_chunk, rem, carry)
    m, l, acc = carry
    return acc / l


def _mla_attn_kernel(q_ref, k_ref, v_ref, o_ref, *, scale, kv_chunk):
    o_ref[0] = _softmax_attend(q_ref[0], k_ref, v_ref, scale, kv_chunk).astype(BF)


def _mla_attention(q, k, v, n):
    b, _, _ = q.shape
    nt = k.shape[1]
    tq = _tile(n, 512)
    dk = 2 * LANE
    return pl.pallas_call(
        functools.partial(_mla_attn_kernel, scale=1.0 / math.sqrt(MLA_NOPE + MLA_ROPE),
                          kv_chunk=min(1024, n)),
        grid=(b, MLA_HEADS, n // tq),
        in_specs=[
            pl.BlockSpec((1, tq, dk), lambda bi, h, i: (bi, i, h)),
            pl.BlockSpec((1, nt, dk), lambda bi, h, i: (bi, 0, h)),
            pl.BlockSpec((1, nt, MLA_V), lambda bi, h, i: (bi, 0, h)),
        ],
        out_specs=pl.BlockSpec((1, tq, MLA_V), lambda bi, h, i: (bi, i, h)),
        out_shape=jax.ShapeDtypeStruct((b, n, MLA_HEADS * MLA_V), BF),
        compiler_params=_cparams("parallel", "parallel", "parallel"),
        name="mla_attention",
    )(q, k, v)


def _diff_attn_kernel(q1_ref, q2_ref, k1_ref, k2_ref, v_ref, lq1_ref, lk1_ref, lq2_ref, lk2_ref,
                      g_ref, o_ref, *, scale, lam_init, kv_chunk):
    o1 = _softmax_attend(q1_ref[0], k1_ref, v_ref, scale, kv_chunk)
    o2 = _softmax_attend(q2_ref[0], k2_ref, v_ref, scale, kv_chunk)
    lam = (jnp.exp(jnp.sum(lq1_ref[...] * lk1_ref[...], axis=-1, keepdims=True))
           - jnp.exp(jnp.sum(lq2_ref[...] * lk2_ref[...], axis=-1, keepdims=True)) + lam_init)
    od = o1 - lam * o2
    o_ref[0] = (_rms_gain(od, g_ref[...]) * (1.0 - lam_init)).astype(BF)


def _diff_attention(qkv, lq1, lk1, lq2, lk2, subln, n, layer_idx):
    b, nt, _ = qkv.shape
    tq = _tile(n, 512)
    hq = DIFF_HEADS
    dv = 2 * DIFF_DIM
    lam_init = 0.8 - 0.6 * math.exp(-0.3 * layer_idx)
    v_block0 = 2 * DIFF_QK // dv
    vec = lambda: _const_spec((1, DIFF_DIM))
    return pl.pallas_call(
        functools.partial(_diff_attn_kernel, scale=1.0 / math.sqrt(DIFF_DIM), lam_init=lam_init,
                          kv_chunk=min(1024, n)),
        grid=(b, hq, n // tq),
        in_specs=[
            pl.BlockSpec((1, tq, DIFF_DIM), lambda bi, h, i: (bi, i, h)),
            pl.BlockSpec((1, tq, DIFF_DIM), lambda bi, h, i: (bi, i, hq + h)),
            pl.BlockSpec((1, nt, DIFF_DIM), lambda bi, h, i: (bi, 0, 2 * hq + h)),
            pl.BlockSpec((1, nt, DIFF_DIM), lambda bi, h, i: (bi, 0, 3 * hq + h)),
            pl.BlockSpec((1, nt, dv), lambda bi, h, i: (bi, 0, v_block0 + h)),
            vec(), vec(), vec(), vec(),
            _const_spec((1, dv)),
        ],
        out_specs=pl.BlockSpec((1, tq, dv), lambda bi, h, i: (bi, i, h)),
        out_shape=jax.ShapeDtypeStruct((b, n, hq * dv), BF),
        compiler_params=_cparams("parallel", "parallel", "parallel"),
        name="diff_attention",
    )(qkv, qkv, qkv, qkv, qkv, lq1, lk1, lq2, lk2, subln)


def _mixer_out_kernel(a1_ref, a2_ref, w_ref, x_ref, g1_ref, gn_ref, sc_ref, sh_ref, rw_ref, rwt_ref,
                      xo_ref, hpk_ref, afft_ref):
    k1 = a1_ref.shape[2]
    y = (jnp.dot(a1_ref[0], w_ref[:k1, :], preferred_element_type=F32)
         + jnp.dot(a2_ref[0], w_ref[k1:, :], preferred_element_type=F32))
    xl = x_ref[0] + g1_ref[0] * y
    xo_ref[0] = xl
    hb = _norm_mod(xl, gn_ref[...], sc_ref[0], sh_ref[0]).astype(BF)
    d = hb.shape[1]
    lane = lax.broadcasted_iota(I32, (1, LANE), 1)
    lg = jnp.dot(hb, rw_ref[...], preferred_element_type=F32)
    lg = jnp.where(lane < N_EXPERTS, lg, -jnp.inf)
    ex = jnp.exp(lg - jnp.max(lg, axis=-1, keepdims=True))
    aff = ex / jnp.sum(ex, axis=-1, keepdims=True)
    lgt = lax.dot_general(rwt_ref[...], hb, (((1,), (1,)), ((), ())), preferred_element_type=F32)
    ext = jnp.exp(lgt - jnp.max(lgt, axis=0, keepdims=True))
    afft_ref[0] = ext / jnp.sum(ext, axis=0, keepdims=True)
    hpk_ref[0, :, :d] = hb.astype(F32)
    hpk_ref[0, :, d:] = aff


def _mixer_out(a1, a2, w, x, g1, gn, sc, sh, rw, rwt):
    b, n, d = x.shape
    k1, k2 = a1.shape[2], a2.shape[2]
    tm = _tile(n, 512)
    return pl.pallas_call(
        _mixer_out_kernel,
        grid=(b, n // tm),
        in_specs=[
            pl.BlockSpec((1, tm, k1), lambda bi, i: (bi, i, 0)),
            pl.BlockSpec((1, tm, k2), lambda bi, i: (bi, i, 0)),
            _const_spec((k1 + k2, d)),
            pl.BlockSpec((1, tm, d), lambda bi, i: (bi, i, 0)),
            pl.BlockSpec((1, 1, d), lambda bi, i: (bi, 0, 0)),
            _const_spec((1, d)),
            pl.BlockSpec((1, 1, d), lambda bi, i: (bi, 0, 0)),
            pl.BlockSpec((1, 1, d), lambda bi, i: (bi, 0, 0)),
            _const_spec((d, LANE)),
            _const_spec((N_EXPERTS, d)),
        ],
        out_specs=[
            pl.BlockSpec((1, tm, d), lambda bi, i: (bi, i, 0)),
            pl.BlockSpec((1, tm, d + AFF_W), lambda bi, i: (bi, i, 0)),
            pl.BlockSpec((1, N_EXPERTS, tm), lambda bi, i: (bi, 0, i)),
        ],
        out_shape=[
            jax.ShapeDtypeStruct((b, n, d), F32),
            jax.ShapeDtypeStruct((b, n, d + AFF_W), F32),
            jax.ShapeDtypeStruct((b, N_EXPERTS, n), F32),
        ],
        compiler_params=_cparams("parallel", "parallel"),
        name="mixer_out",
    )(a1, a2, w, x, g1, gn, sc, sh, rw, rwt)


def _route_kernel(afft_ref, tv_ref, tri_ref, idx_ref, pos_ref, offs_ref, cnt_scr, *, cap):
    a = afft_ref[0]
    ne, n = a.shape

    def count_ge(th):
        return jnp.sum(jnp.where(a >= th, 1.0, 0.0), axis=1, keepdims=True)

    def bracket_values(lo, hi):
        vmin = jnp.min(jnp.where(a >= lo, a, jnp.inf), axis=1, keepdims=True)
        vmax = jnp.max(jnp.where(a < hi, a, -jnp.inf), axis=1, keepdims=True)
        return vmin, vmax

    def unresolved(state):
        vmin, vmax = bracket_values(*state)
        return jnp.max(jnp.where(vmin < vmax, 1.0, 0.0)) > 0.5

    def bisect(state):
        lo, hi = state
        mid = 0.5 * (lo + hi)
        keep = count_ge(mid) >= cap
        return jnp.where(keep, mid, lo), jnp.where(keep, hi, mid)

    lo, hi = lax.while_loop(unresolved, bisect, (jnp.zeros((ne, 1), F32), jnp.full((ne, 1), 2.0, F32)))
    thr, _ = bracket_values(lo, hi)
    gt = a > thr
    eq = a == thr
    need = cap - jnp.sum(jnp.where(gt, 1.0, 0.0), axis=1, keepdims=True)

    def prefix_count(mask, emit_offsets):
        ones = jnp.where(mask, 1.0, 0.0)
        off = jnp.zeros((ne, 1), F32)
        lane = lax.broadcasted_iota(I32, (1, LANE), 1)
        offs = jnp.zeros((ne, LANE), F32)
        for c in range(n // COUNT_CHUNK):
            sl = slice(c * COUNT_CHUNK, (c + 1) * COUNT_CHUNK)
            if emit_offsets:
                offs = jnp.where(lane == c, off, offs)
            cs = jnp.dot(ones[:, sl].astype(BF), tri_ref[...], preferred_element_type=F32) + off
            cnt_scr[:, sl] = cs
            off = cs[:, COUNT_CHUNK - 1:COUNT_CHUNK]
        if emit_offsets:
            offs = jnp.where(lane == n // COUNT_CHUNK, off, offs)
        return cnt_scr[...], offs

    cum_eq, _ = prefix_count(eq, False)
    sel = gt | (eq & (cum_eq <= need))
    cum_sel, offs = prefix_count(sel, True)
    pos = jnp.where(sel, cum_sel.astype(I32) - 1, -1)
    pos_ref[0] = pos
    offs_ref[0] = offs.astype(I32)
    slot = lax.broadcasted_iota(I32, (cap, n), 0)
    for e in range(ne):
        onehot = jnp.where(pos[e:e + 1, :] == slot, 1.0, 0.0).astype(BF)
        r = lax.dot_general(tv_ref[...], onehot, (((1,), (1,)), ((), ())), preferred_element_type=F32)
        idx_ref[0, e:e + 1, :] = (r[0:1, :] * 64.0 + r[1:2, :]).astype(I32)


def _route(afft):
    b, ne, n = afft.shape
    cap = (EC_CAPACITY * n) // N_EXPERTS
    assert n % COUNT_CHUNK == 0 and n // COUNT_CHUNK < LANE and n <= 64 * 256
    t = jnp.arange(n)
    tv = jnp.zeros((8, n), F32).at[0].set((t // 64).astype(F32)).at[1].set((t % 64).astype(F32)).astype(BF)
    tri = (jnp.arange(COUNT_CHUNK)[:, None] <= jnp.arange(COUNT_CHUNK)[None, :]).astype(BF)
    return pl.pallas_call(
        functools.partial(_route_kernel, cap=cap),
        grid=(b,),
        in_specs=[
            pl.BlockSpec((1, ne, n), lambda bi: (bi, 0, 0)),
            pl.BlockSpec((8, n), lambda bi: (0, 0)),
            pl.BlockSpec((COUNT_CHUNK, COUNT_CHUNK), lambda bi: (0, 0)),
        ],
        out_specs=[
            pl.BlockSpec((1, ne, cap), lambda bi: (bi, 0, 0)),
            pl.BlockSpec((1, ne, n), lambda bi: (bi, 0, 0)),
            pl.BlockSpec((1, ne, LANE), lambda bi: (bi, 0, 0)),
        ],
        out_shape=[
            jax.ShapeDtypeStruct((b, ne, cap), I32),
            jax.ShapeDtypeStruct((b, ne, n), I32),
            jax.ShapeDtypeStruct((b, ne, LANE), I32),
        ],
        scratch_shapes=[pltpu.VMEM((ne, n), F32)],
        compiler_params=_cparams("parallel"),
        name="route",
    )(afft, tv, tri)


def _ffn_kernel(idx_ref, hpk_hbm, wg_ref, wu_ref, wd_ref, y_ref, land, acc, sem, *, m, row_chunk):
    e = pl.program_id(0)
    hf = pl.program_id(1)
    f = pl.program_id(2)
    nf = pl.num_programs(2)
    base = (e * pl.num_programs(1) + hf) * m

    def row_copy(j):
        row = idx_ref[base + j]
        return pltpu.make_async_copy(hpk_hbm.at[pl.ds(row, 1), :], land.at[pl.ds(j, 1), :], sem)

    @pl.when(f == 0)
    def _():
        def issue(j, c):
            row_copy(j).start()
            return c
        lax.fori_loop(0, m, issue, 0)

        def drain(j, c):
            row_copy(j).wait()
            return c
        lax.fori_loop(0, m, drain, 0)

    d = wg_ref.shape[2]
    wg = wg_ref[0, 0].astype(BF)
    wu = wu_ref[0, 0].astype(BF)
    wd = wd_ref[0, 0].astype(BF)
    for rc in range(m // row_chunk):
        rows = slice(rc * row_chunk, (rc + 1) * row_chunk)
        xe = land[rows, :d].astype(BF)
        g = jnp.dot(xe, wg, preferred_element_type=F32)
        u = jnp.dot(xe, wu, preferred_element_type=F32)
        hid = (g * jax.nn.sigmoid(g) * u).astype(BF)
        part = jnp.dot(hid, wd, preferred_element_type=F32)

        @pl.when(f == 0)
        def _():
            acc[rows, :] = part

        @pl.when(f > 0)
        def _():
            acc[rows, :] += part

    @pl.when(f == nf - 1)
    def _():
        lane = lax.broadcasted_iota(I32, (1, LANE), 1)
        gate = jnp.sum(jnp.where(lane == e, land[:, d:], 0.0), axis=1, keepdims=True)
        y_ref[0] = (acc[...] * gate).astype(BF)


def _expert_ffn(idx_flat, hpk, w_gate, w_up, w_down, layer, m_total):
    _, ne, d, ff = w_gate.shape
    n_half = 2
    m = m_total // n_half
    tf = 256
    row_chunk = _tile(m, 512)
    return pl.pallas_call(
        functools.partial(_ffn_kernel, m=m, row_chunk=row_chunk),
        grid_spec=pltpu.PrefetchScalarGridSpec(
            num_scalar_prefetch=1,
            grid=(ne, n_half, ff // tf),
            in_specs=[
                pl.BlockSpec(memory_space=pl.ANY),
                pl.BlockSpec((1, 1, d, tf), lambda e, hf, f, idx: (layer, e, 0, f)),
                pl.BlockSpec((1, 1, d, tf), lambda e, hf, f, idx: (layer, e, 0, f)),
                pl.BlockSpec((1, 1, tf, d), lambda e, hf, f, idx: (layer, e, f, 0)),
            ],
            out_specs=pl.BlockSpec((1, m, d), lambda e, hf, f, idx: (e, hf, 0)),
            scratch_shapes=[
                pltpu.VMEM((m, d + AFF_W), F32),
                pltpu.VMEM((m, d), F32),
                pltpu.SemaphoreType.DMA(()),
            ],
        ),
        out_shape=jax.ShapeDtypeStruct((ne, m_total, d), BF),
        compiler_params=_cparams("arbitrary", "arbitrary", "arbitrary"),
        name="expert_ffn",
    )(idx_flat, hpk, w_gate, w_up, w_down)


def _combine_kernel(meta_ref, y_hbm, pos_ref, x_ref, g_ref, *rest, cap, nblk, mode):
    if mode == "final":
        gn_ref, o_ref, ybuf, sems = rest
    else:
        o_ref, ybuf, sems = rest
    b = pl.program_id(0)
    tb = pl.program_id(1)
    m0 = (b * nblk + tb) * 32
    rounds = meta_ref[m0]
    pos = pos_ref[0]
    tm, ne = pos.shape
    lane = lax.broadcasted_iota(I32, (1, LANE), 1)
    upper = lane >= CHUNK_ROWS
    lane_row = lane % CHUNK_ROWS

    def chunk_copy(e, start):
        row0 = pl.multiple_of(b * cap + start, SUBLANE_BF16)
        return pltpu.make_async_copy(
            y_hbm.at[e, pl.ds(row0, CHUNK_ROWS), :],
            ybuf.at[pl.ds(e * CHUNK_ROWS, CHUNK_ROWS), :], sems.at[e])

    def one_round(r, acc):
        first = [meta_ref[m0 + 1 + e] + r * CHUNK_ROWS for e in range(ne)]
        start = [jnp.minimum(fs, cap - CHUNK_ROWS) for fs in first]
        for e in range(ne):
            chunk_copy(e, start[e]).start()
        pieces = []
        for p in range(ne // 2):
            e0, e1 = 2 * p, 2 * p + 1
            pe = jnp.where(upper, pos[:, e1:e1 + 1], pos[:, e0:e0 + 1])
            st = jnp.where(upper, start[e1], start[e0])
            fs = jnp.where(upper, first[e1], first[e0])
            hit = (pe - st == lane_row) & (pe >= fs)
            pieces.append(jnp.where(hit, 1.0, 0.0).astype(BF))
        onehot = jnp.concatenate(pieces, axis=1)
        for e in range(ne):
            chunk_copy(e, start[e]).wait()
        return acc + jnp.dot(onehot, ybuf[...], preferred_element_type=F32)

    d = x_ref.shape[2]
    moe = lax.fori_loop(0, rounds, one_round, jnp.zeros((tm, d), F32))
    xl = x_ref[0] + g_ref[0] * moe
    if mode == "final":
        o_ref[0] = _rms_gain(xl, gn_ref[...])
    else:
        o_ref[0] = xl


def _combine(meta, y, pos_t, x, g2, final_gain):
    b, n, d = x.shape
    ne = y.shape[0]
    cap = y.shape[1] // b
    nblk = n // TOKEN_BLOCK
    assert cap >= CHUNK_ROWS and cap % SUBLANE_BF16 == 0 and 2 * CHUNK_ROWS == LANE
    mode = "final" if final_gain is not None else "plain"
    in_specs = [
        pl.BlockSpec(memory_space=pl.ANY),
        pl.BlockSpec((1, TOKEN_BLOCK, ne), lambda bi, i, meta: (bi, i, 0)),
        pl.BlockSpec((1, TOKEN_BLOCK, d), lambda bi, i, meta: (bi, i, 0)),
        pl.BlockSpec((1, 1, d), lambda bi, i, meta: (bi, 0, 0)),
    ]
    args = [meta, y, pos_t, x, g2]
    if mode == "final":
        in_specs.append(pl.BlockSpec((1, d), lambda bi, i, meta: (0, 0)))
        args.append(final_gain)
    return pl.pallas_call(
        functools.partial(_combine_kernel, cap=cap, nblk=nblk, mode=mode),
        grid_spec=pltpu.PrefetchScalarGridSpec(
            num_scalar_prefetch=1,
            grid=(b, nblk),
            in_specs=in_specs,
            out_specs=pl.BlockSpec((1, TOKEN_BLOCK, d), lambda bi, i, meta: (bi, i, 0)),
            scratch_shapes=[
                pltpu.VMEM((ne * CHUNK_ROWS, d), BF),
                pltpu.SemaphoreType.DMA((ne,)),
            ],
        ),
        out_shape=jax.ShapeDtypeStruct((b, n, d), F32),
        compiler_params=_cparams("arbitrary", "arbitrary"),
        name="moe_combine",
    )(*args)


def _moe(hpk, afft, xl, g2, w_gate, w_up, w_down, layer, final_gain):
    b, n, d = xl.shape
    ne = N_EXPERTS
    idx, pos, offs = _route(afft)
    cap = idx.shape[2]
    rows = idx + (jnp.arange(b, dtype=I32) * n)[:, None, None]
    idx_flat = jnp.transpose(rows, (1, 0, 2)).reshape(-1)
    per_blk = TOKEN_BLOCK // COUNT_CHUNK
    nblk = n // TOKEN_BLOCK
    lo = offs[:, :, 0:nblk * per_blk:per_blk]
    hi = offs[:, :, per_blk:nblk * per_blk + 1:per_blk]
    first = (lo // SUBLANE_BF16) * SUBLANE_BF16
    rounds = jnp.max((hi - first + CHUNK_ROWS - 1) // CHUNK_ROWS, axis=1)
    meta = jnp.concatenate(
        [rounds[:, :, None], jnp.transpose(first, (0, 2, 1)), jnp.zeros((b, nblk, 31 - ne), I32)], axis=2)
    pos_t = jnp.transpose(pos, (0, 2, 1))
    y = _expert_ffn(idx_flat, hpk.reshape(b * n, d + AFF_W), w_gate, w_up, w_down, layer, b * cap)
    return _combine(meta.reshape(-1), y, pos_t, xl, g2, final_gain)


def _odd_pool_in_kernel(a_ref, w_ref, zp_ref):
    zp_ref[0] = jnp.dot(a_ref[0], w_ref[...], preferred_element_type=F32)


def _odd_fourier_in_kernel(a_ref, w_ref, cs_ref, uv_ref, *, groups):
    acc = jnp.dot(a_ref[0], w_ref[...], preferred_element_type=F32)
    gw = acc.shape[1] // groups
    for g in range(groups):
        zg = acc[:, g * gw:(g + 1) * gw].astype(BF)
        uv_ref[0, :, g * 2 * gw:(g + 1) * 2 * gw] = jnp.dot(
            zg, cs_ref[...], preferred_element_type=F32).astype(BF)


def _odd_in(h, w_pool, w_fourier, cs_c, groups):
    b, n, d = h.shape
    half = w_pool.shape[1]
    tm = _tile(n, 1024)
    a_spec = pl.BlockSpec((1, tm, d), lambda bi, i: (bi, i, 0))
    zp = pl.pallas_call(
        _odd_pool_in_kernel,
        grid=(b, n // tm),
        in_specs=[a_spec, _const_spec((d, half))],
        out_specs=pl.BlockSpec((1, tm, half), lambda bi, i: (bi, i, 0)),
        out_shape=jax.ShapeDtypeStruct((b, n, half), F32),
        compiler_params=_cparams("parallel", "parallel"),
        name="odd_pool_in",
    )(h, w_pool)
    uv = pl.pallas_call(
        functools.partial(_odd_fourier_in_kernel, groups=groups),
        grid=(b, n // tm),
        in_specs=[a_spec, _const_spec((d, half)), _const_spec(cs_c.shape)],
        out_specs=pl.BlockSpec((1, tm, 2 * half), lambda bi, i: (bi, i, 0)),
        out_shape=jax.ShapeDtypeStruct((b, n, 2 * half), BF),
        compiler_params=_cparams("parallel", "parallel"),
        name="odd_fourier_in",
    )(h, w_fourier, cs_c)
    return zp, uv


def _pool_kernel(prev_ref, cur_ref, next_ref, w_ref, s_ref, o_ref, *, n):
    i = pl.program_id(1)
    tm = cur_ref.shape[1]
    halo = prev_ref.shape[1]
    cur = cur_ref[0]
    prev = jnp.where(i > 0, prev_ref[0], 0.0)
    nxt = jnp.where(i < pl.num_programs(1) - 1, next_ref[0], 0.0)
    ext = jnp.concatenate([prev, cur, nxt], axis=0)
    t = i * tm + lax.broadcasted_iota(I32, (tm, 1), 0)
    gw = cur.shape[1] // len(POOL_WINDOWS)
    for g, win in enumerate(POOL_WINDOWS):
        hw = win // 2
        cols = slice(g * gw, (g + 1) * gw)
        tot = jnp.zeros((tm, gw), F32)
        for j in range(-hw, hw):
            tot = tot + ext[halo + j:halo + j + tm, cols]
        cnt = (jnp.minimum(t + hw, n) - jnp.maximum(t - hw, 0)).astype(F32)
        pooled = (tot / cnt - cur[:, cols]).astype(BF)
        o_ref[0, :, cols] = (jnp.dot(pooled, w_ref[g], preferred_element_type=F32) * s_ref[:, cols]).astype(BF)


def _pool_mix(zp, pool_w, pool_scale):
    b, n, width = zp.shape
    tm = _tile(n, 512)
    halo = 8
    assert max(POOL_WINDOWS) // 2 <= halo
    r = tm // halo
    nb = n // tm
    return pl.pallas_call(
        functools.partial(_pool_kernel, n=n),
        grid=(b, nb),
        in_specs=[
            pl.BlockSpec((1, halo, width), lambda bi, i: (bi, jnp.maximum(i * r - 1, 0), 0)),
            pl.BlockSpec((1, tm, width), lambda bi, i: (bi, i, 0)),
            pl.BlockSpec((1, halo, width), lambda bi, i: (bi, jnp.minimum((i + 1) * r, n // halo - 1), 0)),
            _const_spec(pool_w.shape),
            _const_spec((1, width)),
        ],
        out_specs=pl.BlockSpec((1, tm, width), lambda bi, i: (bi, i, 0)),
        out_shape=jax.ShapeDtypeStruct((b, n, width), BF),
        compiler_params=_cparams("parallel", "parallel"),
        name="pool_mix",
    )(zp, zp, zp, pool_w, pool_scale)


def _fourier_kernel(cn_ref, sn_ref, u_ref, v_ref, w_ref, o_ref):
    spec = (jnp.dot(cn_ref[...], u_ref[0], preferred_element_type=F32)
            - jnp.dot(sn_ref[...], v_ref[0], preferred_element_type=F32))
    o_ref[0] = jnp.dot(spec.astype(BF), w_ref[0], preferred_element_type=F32).astype(BF)


def _fourier_mix(uv, cn, sn, fw):
    b, n, w2 = uv.shape
    groups, gw, _ = fw.shape
    tj = _tile(n, 1024)
    return pl.pallas_call(
        _fourier_kernel,
        grid=(n // tj, b, groups),
        in_specs=[
            pl.BlockSpec((tj, n), lambda j, bi, g: (j, 0)),
            pl.BlockSpec((tj, n), lambda j, bi, g: (j, 0)),
            pl.BlockSpec((1, n, gw), lambda j, bi, g: (bi, 0, 2 * g)),
            pl.BlockSpec((1, n, gw), lambda j, bi, g: (bi, 0, 2 * g + 1)),
            pl.BlockSpec((1, gw, gw), lambda j, bi, g: (g, 0, 0)),
        ],
        out_specs=pl.BlockSpec((1, tj, gw), lambda j, bi, g: (bi, j, g)),
        out_shape=jax.ShapeDtypeStruct((b, n, groups * gw), BF),
        compiler_params=_cparams("parallel", "parallel", "parallel"),
        name="fourier_mix",
    )(cn, sn, uv, uv, fw)


def _dft_tables(n):
    k = jnp.arange(n, dtype=I32)
    prod = (k[:, None] * k[None, :]) % n
    ang = prod.astype(F32) * (2.0 * math.pi / n)
    s = 1.0 / math.sqrt(n)
    return jnp.cos(ang) * s, jnp.sin(ang) * s


def kernel(x, c, ctx, c_ctx, mod_w, mod_b, norm_mix, norm_ffn, even_w_in, mla_q_norm, mla_w_uq, mla_kv_norm, mla_w_ukv, diff_lambda_q1, diff_lambda_k1, diff_lambda_q2, diff_lambda_k2, diff_subln, even_w_out, odd_w_in, pool_w, pool_scale, fourier_w, odd_w_out, router_w, expert_w_gate, expert_w_up, expert_w_down, final_norm):
    b, n, d = x.shape
    n_ctx = ctx.shape[1]

    c8 = jnp.concatenate([c, c_ctx[None, :], jnp.zeros((8 - b - 1, d), F32)], axis=0)
    mod = _modulation(c8, mod_w, mod_b)

    def mod_rows(i, k, ctx_row=False):
        sl = mod[i, :, k * d:(k + 1) * d]
        return sl[b:b + 1] if ctx_row else sl[:b, None, :]

    def router_mats(i):
        rw = router_w[i]
        rw_pad = jnp.concatenate([rw, jnp.zeros((d, LANE - N_EXPERTS), F32)], axis=1).astype(BF)
        return rw_pad, rw.T.astype(BF)

    i = 0
    w_in = even_w_in[0]
    ql, kvl, rp = MLA_Q_LORA, MLA_KV_LORA, MLA_ROPE
    perm_a = _rope_perm(MLA_ROPE // 4, 64 - MLA_ROPE // 2)
    perm_b = _rope_perm(DIFF_DIM // 4, 0)
    w_lora = jnp.concatenate(
        [w_in[:, :ql + kvl], _take_cols(w_in[:, ql + kvl:ql + kvl + rp], perm_a)], axis=1).astype(BF)
    c0 = ql + kvl + rp
    w_qk = w_in[:, c0:c0 + 2 * DIFF_QK].reshape(d, 2 * DIFF_QK // DIFF_DIM, DIFF_DIM)
    w_qk = jnp.take(w_qk, jnp.array(perm_b, I32), axis=2).reshape(d, 2 * DIFF_QK)
    w_diff = jnp.concatenate([w_qk, w_in[:, c0 + 2 * DIFF_QK:]], axis=1).astype(BF)
    w_uq = mla_w_uq[0].reshape(ql, MLA_HEADS, MLA_NOPE + MLA_ROPE)
    w_uq = jnp.concatenate(
        [w_uq[:, :, :MLA_NOPE],
         jnp.stack([_take_cols(w_uq[:, hd, MLA_NOPE:], perm_a) for hd in range(MLA_HEADS)], axis=1)],
        axis=2).reshape(ql, MLA_HEADS * 2 * LANE).astype(BF)
    w_ukv = mla_w_ukv[0].astype(BF)
    cs_a, sn_a = _rope_tables(n, n_ctx, MLA_ROPE // 4, 64 - MLA_ROPE // 2)
    cs_b, sn_b = _rope_tables(n, n_ctx, DIFF_DIM // 4, 0)

    h_all = _normmod_cat(x, ctx, norm_mix[i][None, :], mod_rows(i, 1), mod_rows(i, 0),
                         mod_rows(i, 1, True), mod_rows(i, 0, True))
    cq, ckv, kr = _proj_lora(h_all, w_lora, mla_q_norm[0][None, :], mla_kv_norm[0][None, :], cs_a, sn_a)
    qkv_d = _proj_diff(h_all, w_diff, cs_b, sn_b)
    q_a = _up_q(cq, w_uq, cs_a, sn_a, n)
    k_a, v_a = _up_kv(ckv, w_ukv, kr)
    o_a = _mla_attention(q_a, k_a, v_a, n)
    o_d = _diff_attention(qkv_d, diff_lambda_q1[0][None, :], diff_lambda_k1[0][None, :],
                          diff_lambda_q2[0][None, :], diff_lambda_k2[0][None, :],
                          diff_subln[0][None, :], n, i)
    rw_pad, rw_t = router_mats(i)
    xl, hpk, afft = _mixer_out(o_a, o_d, even_w_out[0].astype(BF), x, mod_rows(i, 2), norm_ffn[i][None, :],
                               mod_rows(i, 4), mod_rows(i, 3), rw_pad, rw_t)
    xl = _moe(hpk, afft, xl, mod_rows(i, 5), expert_w_gate, expert_w_up, expert_w_down, i, None)

    i = 1
    groups = fourier_w.shape[1]
    gw = fourier_w.shape[2]
    cc, sc_ = _dft_tables(gw)
    cs_c = jnp.concatenate([cc, sc_], axis=1).astype(BF)
    cn, sn = _dft_tables(n)
    h = _normmod(xl, norm_mix[i][None, :], mod_rows(i, 1), mod_rows(i, 0))
    pool_width = pool_w.shape[1] * pool_w.shape[2]
    w_odd = odd_w_in[0].astype(BF)
    zp, uv = _odd_in(h, w_odd[:, :pool_width], w_odd[:, pool_width:], cs_c, groups)
    yp = _pool_mix(zp, pool_w[0].astype(BF), pool_scale[0][None, :])
    yf = _fourier_mix(uv, cn.astype(BF), sn.astype(BF), fourier_w[0].astype(BF))
    rw_pad, rw_t = router_mats(i)
    xl, hpk, afft = _mixer_out(yp, yf, odd_w_out[0].astype(BF), xl, mod_rows(i, 2), norm_ffn[i][None, :],
                               mod_rows(i, 4), mod_rows(i, 3), rw_pad, rw_t)
    return _moe(hpk, afft, xl, mod_rows(i, 5), expert_w_gate, expert_w_up, expert_w_down, i,
                final_norm[None, :])
```

```python
import functools
import math

import jax
import jax.numpy as jnp
from jax import lax
from jax.experimental import pallas as pl
from jax.experimental.pallas import tpu as pltpu

F32 = jnp.float32
BF = jnp.bfloat16
I32 = jnp.int32
U32 = jnp.uint32

GRID_W = 64
ROPE_BASE = 10000.0
EPS = 1e-6
MLA_HEADS = 8
MLA_Q_LORA = 512
MLA_KV_LORA = 512
MLA_NOPE = 128
MLA_ROPE = 64
MLA_V = 128
DIFF_HEADS = 4
DIFF_DIM = 128
DIFF_QK = 2 * DIFF_HEADS * DIFF_DIM
POOL_WINDOWS = (2, 4, 8, 16)
N_EXPERTS = 16
EC_CAPACITY = 2

LANE = 128
SUBLANE_BF16 = 16
VMEM_LIMIT = 56 * 1024 * 1024

TOKEN_BLOCK = 256
CHUNK_ROWS = 64
COUNT_CHUNK = 128
AFF_W = LANE


def _cparams(*sem):
    return pltpu.CompilerParams(dimension_semantics=sem, vmem_limit_bytes=VMEM_LIMIT)


def _tile(total, target, mult=SUBLANE_BF16):
    best = None
    for t in range(mult, min(total, target) + 1, mult):
        if total % t == 0:
            best = t
    assert best is not None, (total, target)
    return best


def _const_spec(shape):
    nd = len(shape)
    return pl.BlockSpec(shape, lambda *_: (0,) * nd, pipeline_mode=pl.Buffered(1))


def _mod_kernel(c_ref, w_ref, b_ref, o_ref):
    cv = c_ref[...]
    s = (cv * jax.nn.sigmoid(cv)).astype(BF)
    o_ref[0] = jnp.dot(s, w_ref[0].astype(BF), preferred_element_type=F32) + b_ref[0]


def _modulation(c8, mod_w, mod_b):
    depth, d, n6 = mod_w.shape
    tn = 1024
    return pl.pallas_call(
        _mod_kernel,
        grid=(depth, n6 // tn),
        in_specs=[
            pl.BlockSpec((8, d), lambda i, j: (0, 0)),
            pl.BlockSpec((1, d, tn), lambda i, j: (i, 0, j)),
            pl.BlockSpec((1, 1, tn), lambda i, j: (i, 0, j)),
        ],
        out_specs=pl.BlockSpec((1, 8, tn), lambda i, j: (i, 0, j)),
        out_shape=jax.ShapeDtypeStruct((depth, 8, n6), F32),
        compiler_params=_cparams("parallel", "parallel"),
        name="modulation",
    )(c8, mod_w, mod_b.reshape(depth, 1, n6))


def _norm_mod(xf, gain, scale, shift):
    ms = jnp.mean(xf * xf, axis=-1, keepdims=True)
    y = xf * lax.rsqrt(ms + EPS) * gain
    return y * (1.0 + scale) + shift


def _normmod_cat_kernel(x_ref, ctx_ref, g_ref, sc_ref, sh_ref, scc_ref, shc_ref, o_ref, *, n_lat_blocks):
    i = pl.program_id(1)

    @pl.when(i < n_lat_blocks)
    def _():
        o_ref[0] = _norm_mod(x_ref[0], g_ref[...], sc_ref[0], sh_ref[0]).astype(BF)

    @pl.when(i >= n_lat_blocks)
    def _():
        o_ref[0] = _norm_mod(ctx_ref[0], g_ref[...], scc_ref[...], shc_ref[...]).astype(BF)


def _normmod_cat(x, ctx, gain, sc, sh, scc, shc):
    b, n, d = x.shape
    n_ctx = ctx.shape[1]
    tm = math.gcd(n, n_ctx)
    tm = _tile(tm, 512)
    nl, nc = n // tm, n_ctx // tm
    return pl.pallas_call(
        functools.partial(_normmod_cat_kernel, n_lat_blocks=nl),
        grid=(b, nl + nc),
        in_specs=[
            pl.BlockSpec((1, tm, d), lambda bi, i: (bi, jnp.minimum(i, nl - 1), 0)),
            pl.BlockSpec((1, tm, d), lambda bi, i: (bi, jnp.maximum(i - nl, 0), 0)),
            pl.BlockSpec((1, d), lambda bi, i: (0, 0)),
            pl.BlockSpec((1, 1, d), lambda bi, i: (bi, 0, 0)),
            pl.BlockSpec((1, 1, d), lambda bi, i: (bi, 0, 0)),
            pl.BlockSpec((1, d), lambda bi, i: (0, 0)),
            pl.BlockSpec((1, d), lambda bi, i: (0, 0)),
        ],
        out_specs=pl.BlockSpec((1, tm, d), lambda bi, i: (bi, i, 0)),
        out_shape=jax.ShapeDtypeStruct((b, n + n_ctx, d), BF),
        compiler_params=_cparams("parallel", "parallel"),
        name="normmod_cat",
    )(x, ctx, gain, sc, sh, scc, shc)


def _normmod_kernel(x_ref, g_ref, sc_ref, sh_ref, o_ref):
    o_ref[0] = _norm_mod(x_ref[0], g_ref[...], sc_ref[0], sh_ref[0]).astype(BF)


def _normmod(x, gain, sc, sh):
    b, n, d = x.shape
    tm = _tile(n, 512)
    return pl.pallas_call(
        _normmod_kernel,
        grid=(b, n // tm),
        in_specs=[
            pl.BlockSpec((1, tm, d), lambda bi, i: (bi, i, 0)),
            pl.BlockSpec((1, d), lambda bi, i: (0, 0)),
            pl.BlockSpec((1, 1, d), lambda bi, i: (bi, 0, 0)),
            pl.BlockSpec((1, 1, d), lambda bi, i: (bi, 0, 0)),
        ],
        out_specs=pl.BlockSpec((1, tm, d), lambda bi, i: (bi, i, 0)),
        out_shape=jax.ShapeDtypeStruct((b, n, d), BF),
        compiler_params=_cparams("parallel", "parallel"),
        name="normmod",
    )(x, gain, sc, sh)


def _rope(xb, cs, sn):
    return xb * cs + pltpu.roll(xb, LANE // 2, 1) * sn


def _rope_tables(n, n_ctx, quarter, pad):
    t = jnp.arange(n)
    row = (t // GRID_W).astype(F32)
    col = (t % GRID_W).astype(F32)
    inv_freq = ROPE_BASE ** (-jnp.arange(quarter, dtype=F32) / quarter)
    ang = jnp.concatenate([row[:, None] * inv_freq, col[:, None] * inv_freq], axis=1)
    cos, sin = jnp.cos(ang), jnp.sin(ang)
    zpad = jnp.zeros((n, pad), F32)
    cs = jnp.concatenate([cos, zpad, cos, zpad], axis=1)
    sn = jnp.concatenate([-sin, zpad, sin, zpad], axis=1)
    cs = jnp.concatenate([cs, jnp.ones((n_ctx, LANE), F32)], axis=0)
    sn = jnp.concatenate([sn, jnp.zeros((n_ctx, LANE), F32)], axis=0)
    return cs, sn


def _rope_perm(quarter, pad):
    src = []
    for j in range(2):
        for a in range(2):
            for f in range(quarter):
                src.append(a * 2 * quarter + j * quarter + f)
        src.extend([-1] * pad)
    return src


def _take_cols(w, src):
    wz = jnp.concatenate([w, jnp.zeros((w.shape[0], 1), w.dtype)], axis=1)
    idx = jnp.array([s if s >= 0 else w.shape[1] for s in src], I32)
    return jnp.take(wz, idx, axis=1)


def _rms_gain(v, g):
    ms = jnp.mean(v * v, axis=-1, keepdims=True)
    return v * lax.rsqrt(ms + EPS) * g


def _proj_lora_kernel(a_ref, w_ref, gq_ref, gkv_ref, cs_ref, sn_ref, cq_ref, ckv_ref, kr_ref):
    acc = jnp.dot(a_ref[0], w_ref[...], preferred_element_type=F32)
    ql, kvl = MLA_Q_LORA, MLA_KV_LORA
    cq_ref[0] = _rms_gain(acc[:, :ql], gq_ref[...]).astype(BF)
    ckv_ref[0] = _rms_gain(acc[:, ql:ql + kvl], gkv_ref[...]).astype(BF)
    kr_ref[0] = _rope(acc[:, ql + kvl:], cs_ref[...], sn_ref[...]).astype(BF)


def _proj_lora(h, w, gq, gkv, cs, sn):
    b, nt, d = h.shape
    tm = _tile(nt, 1100)
    nw = w.shape[1]
    return pl.pallas_call(
        _proj_lora_kernel,
        grid=(b, nt // tm),
        in_specs=[
            pl.BlockSpec((1, tm, d), lambda bi, i: (bi, i, 0)),
            _const_spec((d, nw)),
            _const_spec((1, MLA_Q_LORA)),
            _const_spec((1, MLA_KV_LORA)),
            pl.BlockSpec((tm, LANE), lambda bi, i: (i, 0)),
            pl.BlockSpec((tm, LANE), lambda bi, i: (i, 0)),
        ],
        out_specs=[
            pl.BlockSpec((1, tm, MLA_Q_LORA), lambda bi, i: (bi, i, 0)),
            pl.BlockSpec((1, tm, MLA_KV_LORA), lambda bi, i: (bi, i, 0)),
            pl.BlockSpec((1, tm, LANE), lambda bi, i: (bi, i, 0)),
        ],
        out_shape=[
            jax.ShapeDtypeStruct((b, nt, MLA_Q_LORA), BF),
            jax.ShapeDtypeStruct((b, nt, MLA_KV_LORA), BF),
            jax.ShapeDtypeStruct((b, nt, LANE), BF),
        ],
        compiler_params=_cparams("parallel", "parallel"),
        name="proj_lora",
    )(h, w, gq, gkv, cs, sn)


def _proj_diff_kernel(a_ref, w_ref, cs_ref, sn_ref, o_ref, *, n_rope_tiles):
    j = pl.program_id(0)
    acc = jnp.dot(a_ref[0], w_ref[...], preferred_element_type=F32)
    tn = acc.shape[1]

    @pl.when(j < n_rope_tiles)
    def _():
        cs, sn = cs_ref[...], sn_ref[...]
        for blk in range(tn // LANE):
            sl = slice(blk * LANE, (blk + 1) * LANE)
            o_ref[0, :, sl] = _rope(acc[:, sl], cs, sn).astype(BF)

    @pl.when(j >= n_rope_tiles)
    def _():
        o_ref[0] = acc.astype(BF)


def _proj_diff(h, w, cs, sn):
    b, nt, d = h.shape
    tm = _tile(nt, 1100)
    tn = 1024
    nw = w.shape[1]
    return pl.pallas_call(
        functools.partial(_proj_diff_kernel, n_rope_tiles=2 * DIFF_QK // tn),
        grid=(nw // tn, b, nt // tm),
        in_specs=[
            pl.BlockSpec((1, tm, d), lambda j, bi, i: (bi, i, 0)),
            pl.BlockSpec((d, tn), lambda j, bi, i: (0, j)),
            pl.BlockSpec((tm, LANE), lambda j, bi, i: (i, 0)),
            pl.BlockSpec((tm, LANE), lambda j, bi, i: (i, 0)),
        ],
        out_specs=pl.BlockSpec((1, tm, tn), lambda j, bi, i: (bi, i, j)),
        out_shape=jax.ShapeDtypeStruct((b, nt, nw), BF),
        compiler_params=_cparams("parallel", "parallel", "parallel"),
        name="proj_diff",
    )(h, w, cs, sn)


def _up_q_kernel(a_ref, w_ref, cs_ref, sn_ref, o_ref):
    acc = jnp.dot(a_ref[0], w_ref[...], preferred_element_type=F32)
    cs, sn = cs_ref[...], sn_ref[...]
    for hd in range(MLA_HEADS):
        c0 = hd * 2 * LANE
        o_ref[0, :, c0:c0 + LANE] = acc[:, c0:c0 + LANE].astype(BF)
        o_ref[0, :, c0 + LANE:c0 + 2 * LANE] = _rope(acc[:, c0 + LANE:c0 + 2 * LANE], cs, sn).astype(BF)


def _up_q(cq, w, cs, sn, n):
    b, nt, k = cq.shape
    tm = _tile(n, 1024)
    nw = w.shape[1]
    return pl.pallas_call(
        _up_q_kernel,
        grid=(b, n // tm),
        in_specs=[
            pl.BlockSpec((1, tm, k), lambda bi, i: (bi, i, 0)),
            _const_spec((k, nw)),
            pl.BlockSpec((tm, LANE), lambda bi, i: (i, 0)),
            pl.BlockSpec((tm, LANE), lambda bi, i: (i, 0)),
        ],
        out_specs=pl.BlockSpec((1, tm, nw), lambda bi, i: (bi, i, 0)),
        out_shape=jax.ShapeDtypeStruct((b, n, nw), BF),
        compiler_params=_cparams("parallel", "parallel"),
        name="up_q",
    )(cq, w, cs, sn)


def _up_kv_kernel(a_ref, w_ref, kr_ref, k_ref, v_ref):
    acc = jnp.dot(a_ref[0], w_ref[...], preferred_element_type=F32)
    kr = kr_ref[0]
    for hd in range(MLA_HEADS):
        c0 = hd * 2 * LANE
        k_ref[0, :, c0:c0 + LANE] = acc[:, c0:c0 + LANE].astype(BF)
        k_ref[0, :, c0 + LANE:c0 + 2 * LANE] = kr
        v_ref[0, :, hd * LANE:(hd + 1) * LANE] = acc[:, c0 + LANE:c0 + 2 * LANE].astype(BF)


def _up_kv(ckv, w, kr):
    b, nt, k = ckv.shape
    tm = _tile(nt, 1100)
    nw = w.shape[1]
    return pl.pallas_call(
        _up_kv_kernel,
        grid=(b, nt // tm),
        in_specs=[
            pl.BlockSpec((1, tm, k), lambda bi, i: (bi, i, 0)),
            _const_spec((k, nw)),
            pl.BlockSpec((1, tm, LANE), lambda bi, i: (bi, i, 0)),
        ],
        out_specs=[
            pl.BlockSpec((1, tm, nw), lambda bi, i: (bi, i, 0)),
            pl.BlockSpec((1, tm, MLA_HEADS * MLA_V), lambda bi, i: (bi, i, 0)),
        ],
        out_shape=[
            jax.ShapeDtypeStruct((b, nt, nw), BF),
            jax.ShapeDtypeStruct((b, nt, MLA_HEADS * MLA_V), BF),
        ],
        compiler_params=_cparams("parallel", "parallel"),
        name="up_kv",
    )(ckv, w, kr)


def _softmax_attend(qs, load_k, load_v, nk, dv, scale, kv_chunk):
    tq = qs[0].shape[0]
    c_exp = scale * math.log2(math.e)

    def step(c0, size, carry):
        out = []
        for i, (q, (m, l, acc)) in enumerate(zip(qs, carry)):
            s = lax.dot_general(q, load_k(i, c0, size), (((1,), (1,)), ((), ())),
                                preferred_element_type=F32)
            m_new = jnp.maximum(m, jnp.max(s, axis=-1, keepdims=True))
            alpha = jnp.exp2((m - m_new) * c_exp)
            p = jnp.exp2((s - m_new) * c_exp)
            l = alpha * l + jnp.sum(p, axis=-1, keepdims=True)
            acc = alpha * acc + jnp.dot(p.astype(BF), load_v(i, c0, size), preferred_element_type=F32)
            out.append((m_new, l, acc))
        return tuple(out)

    carry = tuple((jnp.full((tq, 1), -jnp.inf, F32), jnp.zeros((tq, 1), F32), jnp.zeros((tq, dv), F32))
                  for _ in qs)
    n_full = nk // kv_chunk
    carry = lax.fori_loop(
        0, n_full, lambda c, cr: step(pl.multiple_of(c * kv_chunk, kv_chunk), kv_chunk, cr), carry)
    rem = nk - n_full * kv_chunk
    if rem:
        carry = step(n_full * kv_chunk, rem, carry)
    return [acc / l for (_, l, acc) in carry]


def _mla_attn_kernel(q_ref, k_ref, v_ref, o_ref, *, scale, kv_chunk, heads):
    dk, dv = q_ref.shape[2] // heads, v_ref.shape[2] // heads
    qs = [q_ref[0, :, h * dk:(h + 1) * dk] for h in range(heads)]
    outs = _softmax_attend(
        qs,
        lambda h, c0, size: k_ref[0, pl.ds(c0, size), h * dk:(h + 1) * dk],
        lambda h, c0, size: v_ref[0, pl.ds(c0, size), h * dv:(h + 1) * dv],
        k_ref.shape[1], dv, scale, kv_chunk)
    for h in range(heads):
        o_ref[0, :, h * dv:(h + 1) * dv] = outs[h].astype(BF)


def _mla_attention(q, k, v, n):
    b, _, _ = q.shape
    nt = k.shape[1]
    tq = _tile(n, 512)
    dk = 2 * LANE
    heads = 2
    return pl.pallas_call(
        functools.partial(_mla_attn_kernel, scale=1.0 / math.sqrt(MLA_NOPE + MLA_ROPE),
                          kv_chunk=min(1024, n), heads=heads),
        grid=(b, MLA_HEADS // heads, n // tq),
        in_specs=[
            pl.BlockSpec((1, tq, heads * dk), lambda bi, h, i: (bi, i, h)),
            pl.BlockSpec((1, nt, heads * dk), lambda bi, h, i: (bi, 0, h)),
            pl.BlockSpec((1, nt, heads * MLA_V), lambda bi, h, i: (bi, 0, h)),
        ],
        out_specs=pl.BlockSpec((1, tq, heads * MLA_V), lambda bi, h, i: (bi, i, h)),
        out_shape=jax.ShapeDtypeStruct((b, n, MLA_HEADS * MLA_V), BF),
        compiler_params=_cparams("parallel", "parallel", "parallel"),
        name="mla_attention",
    )(q, k, v)


def _diff_attn_kernel(q1_ref, q2_ref, k1_ref, k2_ref, v_ref, lq1_ref, lk1_ref, lq2_ref, lk2_ref,
                      g_ref, o_ref, *, scale, lam_init, kv_chunk):
    k_refs = (k1_ref, k2_ref)
    o1, o2 = _softmax_attend(
        [q1_ref[0], q2_ref[0]],
        lambda i, c0, size: k_refs[i][0, pl.ds(c0, size), :],
        lambda i, c0, size: v_ref[0, pl.ds(c0, size), :],
        v_ref.shape[1], v_ref.shape[2], scale, kv_chunk)
    lam = (jnp.exp(jnp.sum(lq1_ref[...] * lk1_ref[...], axis=-1, keepdims=True))
           - jnp.exp(jnp.sum(lq2_ref[...] * lk2_ref[...], axis=-1, keepdims=True)) + lam_init)
    od = o1 - lam * o2
    o_ref[0] = (_rms_gain(od, g_ref[...]) * (1.0 - lam_init)).astype(BF)


def _diff_attention(qkv, lq1, lk1, lq2, lk2, subln, n, layer_idx):
    b, nt, _ = qkv.shape
    tq = _tile(n, 512)
    hq = DIFF_HEADS
    dv = 2 * DIFF_DIM
    lam_init = 0.8 - 0.6 * math.exp(-0.3 * layer_idx)
    v_block0 = 2 * DIFF_QK // dv
    vec = lambda: _const_spec((1, DIFF_DIM))
    return pl.pallas_call(
        functools.partial(_diff_attn_kernel, scale=1.0 / math.sqrt(DIFF_DIM), lam_init=lam_init,
                          kv_chunk=min(1024, n)),
        grid=(b, hq, n // tq),
        in_specs=[
            pl.BlockSpec((1, tq, DIFF_DIM), lambda bi, h, i: (bi, i, h)),
            pl.BlockSpec((1, tq, DIFF_DIM), lambda bi, h, i: (bi, i, hq + h)),
            pl.BlockSpec((1, nt, DIFF_DIM), lambda bi, h, i: (bi, 0, 2 * hq + h)),
            pl.BlockSpec((1, nt, DIFF_DIM), lambda bi, h, i: (bi, 0, 3 * hq + h)),
            pl.BlockSpec((1, nt, dv), lambda bi, h, i: (bi, 0, v_block0 + h)),
            vec(), vec(), vec(), vec(),
            _const_spec((1, dv)),
        ],
        out_specs=pl.BlockSpec((1, tq, dv), lambda bi, h, i: (bi, i, h)),
        out_shape=jax.ShapeDtypeStruct((b, n, hq * dv), BF),
        compiler_params=_cparams("parallel", "parallel", "parallel"),
        name="diff_attention",
    )(qkv, qkv, qkv, qkv, qkv, lq1, lk1, lq2, lk2, subln)


def _mixer_out_kernel(a1_ref, a2_ref, w_ref, x_ref, g1_ref, gn_ref, sc_ref, sh_ref, rw_ref, rwt_ref,
                      xo_ref, hpk_ref, afft_ref):
    k1 = a1_ref.shape[2]
    y = (jnp.dot(a1_ref[0], w_ref[:k1, :], preferred_element_type=F32)
         + jnp.dot(a2_ref[0], w_ref[k1:, :], preferred_element_type=F32))
    xl = x_ref[0] + g1_ref[0] * y
    xo_ref[0] = xl
    hb = _norm_mod(xl, gn_ref[...], sc_ref[0], sh_ref[0]).astype(BF)
    d = hb.shape[1]
    lane = lax.broadcasted_iota(I32, (1, LANE), 1)
    lg = jnp.dot(hb, rw_ref[...], preferred_element_type=F32)
    lg = jnp.where(lane < N_EXPERTS, lg, -jnp.inf)
    ex = jnp.exp(lg - jnp.max(lg, axis=-1, keepdims=True))
    aff = ex / jnp.sum(ex, axis=-1, keepdims=True)
    lgt = lax.dot_general(rwt_ref[...], hb, (((1,), (1,)), ((), ())), preferred_element_type=F32)
    ext = jnp.exp(lgt - jnp.max(lgt, axis=0, keepdims=True))
    afft_ref[0] = ext / jnp.sum(ext, axis=0, keepdims=True)
    hpk_ref[0, :, :d] = hb.astype(F32)
    hpk_ref[0, :, d:] = aff


def _mixer_out(a1, a2, w, x, g1, gn, sc, sh, rw, rwt):
    b, n, d = x.shape
    k1, k2 = a1.shape[2], a2.shape[2]
    tm = _tile(n, 512)
    return pl.pallas_call(
        _mixer_out_kernel,
        grid=(b, n // tm),
        in_specs=[
            pl.BlockSpec((1, tm, k1), lambda bi, i: (bi, i, 0)),
            pl.BlockSpec((1, tm, k2), lambda bi, i: (bi, i, 0)),
            _const_spec((k1 + k2, d)),
            pl.BlockSpec((1, tm, d), lambda bi, i: (bi, i, 0)),
            pl.BlockSpec((1, 1, d), lambda bi, i: (bi, 0, 0)),
            _const_spec((1, d)),
            pl.BlockSpec((1, 1, d), lambda bi, i: (bi, 0, 0)),
            pl.BlockSpec((1, 1, d), lambda bi, i: (bi, 0, 0)),
            _const_spec((d, LANE)),
            _const_spec((N_EXPERTS, d)),
        ],
        out_specs=[
            pl.BlockSpec((1, tm, d), lambda bi, i: (bi, i, 0)),
            pl.BlockSpec((1, tm, d + AFF_W), lambda bi, i: (bi, i, 0)),
            pl.BlockSpec((1, N_EXPERTS, tm), lambda bi, i: (bi, 0, i)),
        ],
        out_shape=[
            jax.ShapeDtypeStruct((b, n, d), F32),
            jax.ShapeDtypeStruct((b, n, d + AFF_W), F32),
            jax.ShapeDtypeStruct((b, N_EXPERTS, n), F32),
        ],
        compiler_params=_cparams("parallel", "parallel"),
        name="mixer_out",
    )(a1, a2, w, x, g1, gn, sc, sh, rw, rwt)


def _route_kernel(afft_ref, tv_ref, tri_ref, idx_ref, pos_ref, offs_ref, cnt_scr, *, cap):
    a = afft_ref[0]
    ne, n = a.shape

    def count_ge(th):
        return jnp.sum(jnp.where(a >= th, 1.0, 0.0), axis=1, keepdims=True)

    def bracket_values(lo, hi):
        vmin = jnp.min(jnp.where(a >= lo, a, jnp.inf), axis=1, keepdims=True)
        vmax = jnp.max(jnp.where(a < hi, a, -jnp.inf), axis=1, keepdims=True)
        return vmin, vmax

    def unresolved(state):
        vmin, vmax = bracket_values(*state)
        return jnp.max(jnp.where(vmin < vmax, 1.0, 0.0)) > 0.5

    def bisect(state):
        lo, hi = state
        mid = 0.5 * (lo + hi)
        keep = count_ge(mid) >= cap
        return jnp.where(keep, mid, lo), jnp.where(keep, hi, mid)

    lo, hi = lax.while_loop(unresolved, bisect, (jnp.zeros((ne, 1), F32), jnp.full((ne, 1), 2.0, F32)))
    thr, _ = bracket_values(lo, hi)
    gt = a > thr
    eq = a == thr
    need = cap - jnp.sum(jnp.where(gt, 1.0, 0.0), axis=1, keepdims=True)

    def prefix_count(mask, emit_offsets):
        ones = jnp.where(mask, 1.0, 0.0)
        off = jnp.zeros((ne, 1), F32)
        lane = lax.broadcasted_iota(I32, (1, LANE), 1)
        offs = jnp.zeros((ne, LANE), F32)
        for c in range(n // COUNT_CHUNK):
            sl = slice(c * COUNT_CHUNK, (c + 1) * COUNT_CHUNK)
            if emit_offsets:
                offs = jnp.where(lane == c, off, offs)
            cs = jnp.dot(ones[:, sl].astype(BF), tri_ref[...], preferred_element_type=F32) + off
            cnt_scr[:, sl] = cs
            off = cs[:, COUNT_CHUNK - 1:COUNT_CHUNK]
        if emit_offsets:
            offs = jnp.where(lane == n // COUNT_CHUNK, off, offs)
        return cnt_scr[...], offs

    cum_eq, _ = prefix_count(eq, False)
    sel = gt | (eq & (cum_eq <= need))
    cum_sel, offs = prefix_count(sel, True)
    pos = jnp.where(sel, cum_sel.astype(I32) - 1, -1)
    pos_ref[0] = pos
    offs_ref[0] = offs.astype(I32)
    slot = lax.broadcasted_iota(I32, (cap, n), 0)
    for e in range(ne):
        onehot = jnp.where(pos[e:e + 1, :] == slot, 1.0, 0.0).astype(BF)
        r = lax.dot_general(tv_ref[...], onehot, (((1,), (1,)), ((), ())), preferred_element_type=F32)
        idx_ref[0, e:e + 1, :] = (r[0:1, :] * 64.0 + r[1:2, :]).astype(I32)


def _route(afft):
    b, ne, n = afft.shape
    cap = (EC_CAPACITY * n) // N_EXPERTS
    assert n % COUNT_CHUNK == 0 and n // COUNT_CHUNK < LANE and n <= 64 * 256
    t = jnp.arange(n)
    tv = jnp.zeros((8, n), F32).at[0].set((t // 64).astype(F32)).at[1].set((t % 64).astype(F32)).astype(BF)
    tri = (jnp.arange(COUNT_CHUNK)[:, None] <= jnp.arange(COUNT_CHUNK)[None, :]).astype(BF)
    return pl.pallas_call(
        functools.partial(_route_kernel, cap=cap),
        grid=(b,),
        in_specs=[
            pl.BlockSpec((1, ne, n), lambda bi: (bi, 0, 0)),
            pl.BlockSpec((8, n), lambda bi: (0, 0)),
            pl.BlockSpec((COUNT_CHUNK, COUNT_CHUNK), lambda bi: (0, 0)),
        ],
        out_specs=[
            pl.BlockSpec((1, ne, cap), lambda bi: (bi, 0, 0)),
            pl.BlockSpec((1, ne, n), lambda bi: (bi, 0, 0)),
            pl.BlockSpec((1, ne, LANE), lambda bi: (bi, 0, 0)),
        ],
        out_shape=[
            jax.ShapeDtypeStruct((b, ne, cap), I32),
            jax.ShapeDtypeStruct((b, ne, n), I32),
            jax.ShapeDtypeStruct((b, ne, LANE), I32),
        ],
        scratch_shapes=[pltpu.VMEM((ne, n), F32)],
        compiler_params=_cparams("parallel"),
        name="route",
    )(afft, tv, tri)


def _ffn_kernel(idx_ref, h_hbm, wg_ref, wu_ref, wd_ref, y_ref, land, xe, hid, gate, sem,
                *, m, row_chunk, n_up):
    e = pl.program_id(0)
    s = pl.program_id(2)
    tile = e * pl.num_programs(1) + pl.program_id(1)
    n_tiles = pl.num_programs(0) * pl.num_programs(1)
    d = wg_ref.shape[2]

    def row_copy(t, j):
        row = idx_ref[t * m + j]
        return pltpu.make_async_copy(h_hbm.at[pl.ds(row, 1), :], land.at[pl.ds(j, 1), :], sem)

    def issue_gather(t):
        def body(j, c):
            row_copy(t, j).start()
            return c
        lax.fori_loop(0, m, body, 0, unroll=8)

    @pl.when(s == 0)
    def _():
        @pl.when(tile == 0)
        def _():
            issue_gather(tile)

        def drain(j, c):
            row_copy(tile, j).wait()
            return c
        lax.fori_loop(0, m, drain, 0, unroll=8)
        lane = lax.broadcasted_iota(I32, (1, LANE), 1)
        for rc in range(m // row_chunk):
            rows = slice(rc * row_chunk, (rc + 1) * row_chunk)
            xe[rows, :] = land[rows, :d].astype(BF)
            gate[rows, :] = jnp.sum(jnp.where(lane == e, land[rows, d:], 0.0), axis=1, keepdims=True)

        @pl.when(tile + 1 < n_tiles)
        def _():
            issue_gather(tile + 1)

    @pl.when(s < n_up)
    def _():
        wg = wg_ref[0, 0].astype(BF)
        wu = wu_ref[0, 0].astype(BF)
        for rc in range(m // row_chunk):
            rows = slice(rc * row_chunk, (rc + 1) * row_chunk)
            x = xe[rows, :]
            g = jnp.dot(x, wg, preferred_element_type=F32)
            u = jnp.dot(x, wu, preferred_element_type=F32)
            hid[s, rows, :] = (g * jax.nn.sigmoid(g) * u).astype(BF)

    @pl.when(s >= n_up)
    def _():
        wd = wd_ref[0, 0].astype(BF)
        tf = hid.shape[2]
        for rc in range(m // row_chunk):
            rows = slice(rc * row_chunk, (rc + 1) * row_chunk)
            acc = jnp.dot(hid[0, rows, :], wd[:tf], preferred_element_type=F32)
            for f in range(1, n_up):
                acc = acc + jnp.dot(hid[f, rows, :], wd[f * tf:(f + 1) * tf], preferred_element_type=F32)
            y_ref[0, rows, :] = (acc * gate[rows, :]).astype(BF)


def _expert_ffn(idx_flat, h_rows, w_gate, w_up, w_down, layer, m_total):
    _, ne, d, ff = w_gate.shape
    n_half = 2
    m = m_total // n_half
    tf = _tile(ff, 512, LANE)
    tn = _tile(d, 512, LANE)
    n_up, n_down = ff // tf, d // tn
    row_chunk = _tile(m, 512)
    up_spec = pl.BlockSpec((1, 1, d, tf), lambda e, hf, s, idx: (layer, e, 0, jnp.minimum(s, n_up - 1)))
    return pl.pallas_call(
        functools.partial(_ffn_kernel, m=m, row_chunk=row_chunk, n_up=n_up),
        grid_spec=pltpu.PrefetchScalarGridSpec(
            num_scalar_prefetch=1,
            grid=(ne, n_half, n_up + n_down),
            in_specs=[
                pl.BlockSpec(memory_space=pl.ANY),
                up_spec,
                up_spec,
                pl.BlockSpec((1, 1, ff, tn), lambda e, hf, s, idx: (layer, e, 0, jnp.maximum(s - n_up, 0))),
            ],
            out_specs=pl.BlockSpec((1, m, tn), lambda e, hf, s, idx: (e, hf, jnp.maximum(s - n_up, 0))),
            scratch_shapes=[
                pltpu.VMEM((m, d + AFF_W), F32),
                pltpu.VMEM((m, d), BF),
                pltpu.VMEM((n_up, m, tf), BF),
                pltpu.VMEM((m, 1), F32),
                pltpu.SemaphoreType.DMA(()),
            ],
        ),
        out_shape=jax.ShapeDtypeStruct((ne, m_total, d), BF),
        compiler_params=_cparams("arbitrary", "arbitrary", "arbitrary"),
        name="expert_ffn",
    )(idx_flat, h_rows, w_gate, w_up, w_down)


def _combine_kernel(meta_ref, y_hbm, pos_ref, x_ref, g_ref, *rest, cap, nblk, mode):
    if mode == "final":
        gn_ref, o_ref, ybuf, sems = rest
    else:
        o_ref, ybuf, sems = rest
    step = pl.program_id(0) * nblk + pl.program_id(1)
    n_steps = pl.num_programs(0) * nblk
    slot = step % 2
    pos = pos_ref[0]
    tm, ne = pos.shape
    lane = lax.broadcasted_iota(I32, (1, LANE), 1)
    upper = lane >= CHUNK_ROWS
    lane_row = lane % CHUNK_ROWS

    def plan(st, r):
        first = [meta_ref[st * 32 + 1 + e] + r * CHUNK_ROWS for e in range(ne)]
        start = [jnp.minimum(fs, cap - CHUNK_ROWS) for fs in first]
        return first, start

    def chunk_copy(st, buf, e, start):
        row0 = pl.multiple_of((st // nblk) * cap + start, SUBLANE_BF16)
        return pltpu.make_async_copy(
            y_hbm.at[e, pl.ds(row0, CHUNK_ROWS), :],
            ybuf.at[buf, pl.ds(e * CHUNK_ROWS, CHUNK_ROWS), :], sems.at[buf, e])

    def fetch(st, buf, r):
        _, start = plan(st, r)
        for e in range(ne):
            chunk_copy(st, buf, e, start[e]).start()

    def place(buf, r):
        first, start = plan(step, r)
        pieces = []
        for p in range(ne // 2):
            e0, e1 = 2 * p, 2 * p + 1
            pe = jnp.where(upper, pos[:, e1:e1 + 1], pos[:, e0:e0 + 1])
            st = jnp.where(upper, start[e1], start[e0])
            fs = jnp.where(upper, first[e1], first[e0])
            hit = (pe - st == lane_row) & (pe >= fs)
            pieces.append(jnp.where(hit, 1.0, 0.0).astype(BF))
        onehot = jnp.concatenate(pieces, axis=1)
        for e in range(ne):
            chunk_copy(step, buf, e, start[e]).wait()
        return jnp.dot(onehot, ybuf[buf], preferred_element_type=F32)

    @pl.when(step == 0)
    def _():
        fetch(step, slot, 0)

    @pl.when(step + 1 < n_steps)
    def _():
        fetch(step + 1, 1 - slot, 0)

    moe = place(slot, 0)

    def extra_round(r, acc):
        fetch(step, slot, r)
        return acc + place(slot, r)

    moe = lax.fori_loop(1, meta_ref[step * 32], extra_round, moe)
    xl = x_ref[0] + g_ref[0] * moe
    if mode == "final":
        o_ref[0] = _rms_gain(xl, gn_ref[...])
    else:
        o_ref[0] = xl


def _combine(meta, y, pos_t, x, g2, final_gain):
    b, n, d = x.shape
    ne = y.shape[0]
    cap = y.shape[1] // b
    nblk = n // TOKEN_BLOCK
    assert cap >= CHUNK_ROWS and cap % SUBLANE_BF16 == 0 and 2 * CHUNK_ROWS == LANE
    mode = "final" if final_gain is not None else "plain"
    in_specs = [
        pl.BlockSpec(memory_space=pl.ANY),
        pl.BlockSpec((1, TOKEN_BLOCK, ne), lambda bi, i, meta: (bi, i, 0)),
        pl.BlockSpec((1, TOKEN_BLOCK, d), lambda bi, i, meta: (bi, i, 0)),
        pl.BlockSpec((1, 1, d), lambda bi, i, meta: (bi, 0, 0)),
    ]
    args = [meta, y, pos_t, x, g2]
    if mode == "final":
        in_specs.append(pl.BlockSpec((1, d), lambda bi, i, meta: (0, 0)))
        args.append(final_gain)
    return pl.pallas_call(
        functools.partial(_combine_kernel, cap=cap, nblk=nblk, mode=mode),
        grid_spec=pltpu.PrefetchScalarGridSpec(
            num_scalar_prefetch=1,
            grid=(b, nblk),
            in_specs=in_specs,
            out_specs=pl.BlockSpec((1, TOKEN_BLOCK, d), lambda bi, i, meta: (bi, i, 0)),
            scratch_shapes=[
                pltpu.VMEM((2, ne * CHUNK_ROWS, d), BF),
                pltpu.SemaphoreType.DMA((2, ne)),
            ],
        ),
        out_shape=jax.ShapeDtypeStruct((b, n, d), F32),
        compiler_params=_cparams("arbitrary", "arbitrary"),
        name="moe_combine",
    )(*args)


def _moe(hpk, afft, xl, g2, w_gate, w_up, w_down, layer, final_gain):
    b, n, d = xl.shape
    ne = N_EXPERTS
    idx, pos, offs = _route(afft)
    cap = idx.shape[2]
    rows = idx + (jnp.arange(b, dtype=I32) * n)[:, None, None]
    idx_flat = jnp.transpose(rows, (1, 0, 2)).reshape(-1)
    per_blk = TOKEN_BLOCK // COUNT_CHUNK
    nblk = n // TOKEN_BLOCK
    lo = offs[:, :, 0:nblk * per_blk:per_blk]
    hi = offs[:, :, per_blk:nblk * per_blk + 1:per_blk]
    first = (lo // SUBLANE_BF16) * SUBLANE_BF16
    rounds = jnp.max((hi - first + CHUNK_ROWS - 1) // CHUNK_ROWS, axis=1)
    meta = jnp.concatenate(
        [rounds[:, :, None], jnp.transpose(first, (0, 2, 1)), jnp.zeros((b, nblk, 31 - ne), I32)], axis=2)
    pos_t = jnp.transpose(pos, (0, 2, 1))
    y = _expert_ffn(idx_flat, hpk.reshape(b * n, d + AFF_W), w_gate, w_up, w_down, layer, b * cap)
    return _combine(meta.reshape(-1), y, pos_t, xl, g2, final_gain)


def _odd_pool_in_kernel(a_ref, w_ref, zp_ref):
    zp_ref[0] = jnp.dot(a_ref[0], w_ref[...], preferred_element_type=F32)


def _odd_fourier_in_kernel(a_ref, w_ref, cs_ref, uv_ref, *, groups):
    acc = jnp.dot(a_ref[0], w_ref[...], preferred_element_type=F32)
    gw = acc.shape[1] // groups
    for g in range(groups):
        zg = acc[:, g * gw:(g + 1) * gw].astype(BF)
        uv_ref[0, :, g * 2 * gw:(g + 1) * 2 * gw] = jnp.dot(
            zg, cs_ref[...], preferred_element_type=F32).astype(BF)


def _odd_in(h, w_pool, w_fourier, cs_c, groups):
    b, n, d = h.shape
    half = w_pool.shape[1]
    tm = _tile(n, 1024)
    a_spec = pl.BlockSpec((1, tm, d), lambda bi, i: (bi, i, 0))
    zp = pl.pallas_call(
        _odd_pool_in_kernel,
        grid=(b, n // tm),
        in_specs=[a_spec, _const_spec((d, half))],
        out_specs=pl.BlockSpec((1, tm, half), lambda bi, i: (bi, i, 0)),
        out_shape=jax.ShapeDtypeStruct((b, n, half), F32),
        compiler_params=_cparams("parallel", "parallel"),
        name="odd_pool_in",
    )(h, w_pool)
    uv = pl.pallas_call(
        functools.partial(_odd_fourier_in_kernel, groups=groups),
        grid=(b, n // tm),
        in_specs=[a_spec, _const_spec((d, half)), _const_spec(cs_c.shape)],
        out_specs=pl.BlockSpec((1, tm, 2 * half), lambda bi, i: (bi, i, 0)),
        out_shape=jax.ShapeDtypeStruct((b, n, 2 * half), BF),
        compiler_params=_cparams("parallel", "parallel"),
        name="odd_fourier_in",
    )(h, w_fourier, cs_c)
    return zp, uv


def _pool_kernel(prev_ref, cur_ref, next_ref, w_ref, s_ref, o_ref, *, n):
    i = pl.program_id(1)
    tm = cur_ref.shape[1]
    halo = prev_ref.shape[1]
    cur = cur_ref[0]
    prev = jnp.where(i > 0, prev_ref[0], 0.0)
    nxt = jnp.where(i < pl.num_programs(1) - 1, next_ref[0], 0.0)
    ext = jnp.concatenate([prev, cur, nxt], axis=0)
    t = i * tm + lax.broadcasted_iota(I32, (tm, 1), 0)
    gw = cur.shape[1] // len(POOL_WINDOWS)
    for g, win in enumerate(POOL_WINDOWS):
        hw = win // 2
        cols = slice(g * gw, (g + 1) * gw)
        tot = jnp.zeros((tm, gw), F32)
        for j in range(-hw, hw):
            tot = tot + ext[halo + j:halo + j + tm, cols]
        cnt = (jnp.minimum(t + hw, n) - jnp.maximum(t - hw, 0)).astype(F32)
        pooled = (tot / cnt - cur[:, cols]).astype(BF)
        o_ref[0, :, cols] = (jnp.dot(pooled, w_ref[g], preferred_element_type=F32) * s_ref[:, cols]).astype(BF)


def _pool_mix(zp, pool_w, pool_scale):
    b, n, width = zp.shape
    tm = _tile(n, 512)
    halo = 8
    assert max(POOL_WINDOWS) // 2 <= halo
    r = tm // halo
    nb = n // tm
    return pl.pallas_call(
        functools.partial(_pool_kernel, n=n),
        grid=(b, nb),
        in_specs=[
            pl.BlockSpec((1, halo, width), lambda bi, i: (bi, jnp.maximum(i * r - 1, 0), 0)),
            pl.BlockSpec((1, tm, width), lambda bi, i: (bi, i, 0)),
            pl.BlockSpec((1, halo, width), lambda bi, i: (bi, jnp.minimum((i + 1) * r, n // halo - 1), 0)),
            _const_spec(pool_w.shape),
            _const_spec((1, width)),
        ],
        out_specs=pl.BlockSpec((1, tm, width), lambda bi, i: (bi, i, 0)),
        out_shape=jax.ShapeDtypeStruct((b, n, width), BF),
        compiler_params=_cparams("parallel", "parallel"),
        name="pool_mix",
    )(zp, zp, zp, pool_w, pool_scale)


def _fourier_kernel(cn_ref, sn_ref, u_ref, v_ref, w_ref, o_ref):
    spec = (jnp.dot(cn_ref[...], u_ref[0], preferred_element_type=F32)
            - jnp.dot(sn_ref[...], v_ref[0], preferred_element_type=F32))
    o_ref[0] = jnp.dot(spec.astype(BF), w_ref[0], preferred_element_type=F32).astype(BF)


def _fourier_mix(uv, cn, sn, fw):
    b, n, w2 = uv.shape
    groups, gw, _ = fw.shape
    tj = _tile(n, 1024)
    return pl.pallas_call(
        _fourier_kernel,
        grid=(n // tj, b, groups),
        in_specs=[
            pl.BlockSpec((tj, n), lambda j, bi, g: (j, 0)),
            pl.BlockSpec((tj, n), lambda j, bi, g: (j, 0)),
            pl.BlockSpec((1, n, gw), lambda j, bi, g: (bi, 0, 2 * g)),
            pl.BlockSpec((1, n, gw), lambda j, bi, g: (bi, 0, 2 * g + 1)),
            pl.BlockSpec((1, gw, gw), lambda j, bi, g: (g, 0, 0)),
        ],
        out_specs=pl.BlockSpec((1, tj, gw), lambda j, bi, g: (bi, j, g)),
        out_shape=jax.ShapeDtypeStruct((b, n, groups * gw), BF),
        compiler_params=_cparams("parallel", "parallel", "parallel"),
        name="fourier_mix",
    )(cn, sn, uv, uv, fw)


def _dft_tables(n):
    r = 64 if n % 64 == 0 else 1
    k = jnp.arange(n, dtype=I32)

    def tables(j):
        ang = ((j[:, None] * k[None, :]) % n).astype(F32) * (2.0 * math.pi / n)
        return jnp.cos(ang), jnp.sin(ang)

    ca, sa = tables(jnp.arange(n // r, dtype=I32) * r)
    cb, sb = tables(jnp.arange(r, dtype=I32))
    s = 1.0 / math.sqrt(n)
    cos = (ca[:, None, :] * cb[None, :, :] - sa[:, None, :] * sb[None, :, :]) * s
    sin = (sa[:, None, :] * cb[None, :, :] + ca[:, None, :] * sb[None, :, :]) * s
    return cos.reshape(n, n), sin.reshape(n, n)


def kernel(x, c, ctx, c_ctx, mod_w, mod_b, norm_mix, norm_ffn, even_w_in, mla_q_norm, mla_w_uq, mla_kv_norm, mla_w_ukv, diff_lambda_q1, diff_lambda_k1, diff_lambda_q2, diff_lambda_k2, diff_subln, even_w_out, odd_w_in, pool_w, pool_scale, fourier_w, odd_w_out, router_w, expert_w_gate, expert_w_up, expert_w_down, final_norm):
    b, n, d = x.shape
    n_ctx = ctx.shape[1]

    c8 = jnp.concatenate([c, c_ctx[None, :], jnp.zeros((8 - b - 1, d), F32)], axis=0)
    mod = _modulation(c8, mod_w, mod_b)

    def mod_rows(i, k, ctx_row=False):
        sl = mod[i, :, k * d:(k + 1) * d]
        return sl[b:b + 1] if ctx_row else sl[:b, None, :]

    def router_mats(i):
        rw = router_w[i]
        rw_pad = jnp.concatenate([rw, jnp.zeros((d, LANE - N_EXPERTS), F32)], axis=1).astype(BF)
        return rw_pad, rw.T.astype(BF)

    i = 0
    w_in = even_w_in[0]
    ql, kvl, rp = MLA_Q_LORA, MLA_KV_LORA, MLA_ROPE
    perm_a = _rope_perm(MLA_ROPE // 4, 64 - MLA_ROPE // 2)
    perm_b = _rope_perm(DIFF_DIM // 4, 0)
    w_lora = jnp.concatenate(
        [w_in[:, :ql + kvl], _take_cols(w_in[:, ql + kvl:ql + kvl + rp], perm_a)], axis=1).astype(BF)
    c0 = ql + kvl + rp
    w_qk = w_in[:, c0:c0 + 2 * DIFF_QK].reshape(d, 2 * DIFF_QK // DIFF_DIM, DIFF_DIM)
    w_qk = jnp.take(w_qk, jnp.array(perm_b, I32), axis=2).reshape(d, 2 * DIFF_QK)
    w_diff = jnp.concatenate([w_qk, w_in[:, c0 + 2 * DIFF_QK:]], axis=1).astype(BF)
    w_uq = mla_w_uq[0].reshape(ql, MLA_HEADS, MLA_NOPE + MLA_ROPE)
    w_uq = jnp.concatenate(
        [w_uq[:, :, :MLA_NOPE],
         jnp.stack([_take_cols(w_uq[:, hd, MLA_NOPE:], perm_a) for hd in range(MLA_HEADS)], axis=1)],
        axis=2).reshape(ql, MLA_HEADS * 2 * LANE).astype(BF)
    w_ukv = mla_w_ukv[0].astype(BF)
    cs_a, sn_a = _rope_tables(n, n_ctx, MLA_ROPE // 4, 64 - MLA_ROPE // 2)
    cs_b, sn_b = _rope_tables(n, n_ctx, DIFF_DIM // 4, 0)

    h_all = _normmod_cat(x, ctx, norm_mix[i][None, :], mod_rows(i, 1), mod_rows(i, 0),
                         mod_rows(i, 1, True), mod_rows(i, 0, True))
    cq, ckv, kr = _proj_lora(h_all, w_lora, mla_q_norm[0][None, :], mla_kv_norm[0][None, :], cs_a, sn_a)
    qkv_d = _proj_diff(h_all, w_diff, cs_b, sn_b)
    q_a = _up_q(cq, w_uq, cs_a, sn_a, n)
    k_a, v_a = _up_kv(ckv, w_ukv, kr)
    o_a = _mla_attention(q_a, k_a, v_a, n)
    o_d = _diff_attention(qkv_d, diff_lambda_q1[0][None, :], diff_lambda_k1[0][None, :],
                          diff_lambda_q2[0][None, :], diff_lambda_k2[0][None, :],
                          diff_subln[0][None, :], n, i)
    rw_pad, rw_t = router_mats(i)
    xl, hpk, afft = _mixer_out(o_a, o_d, even_w_out[0].astype(BF), x, mod_rows(i, 2), norm_ffn[i][None, :],
                               mod_rows(i, 4), mod_rows(i, 3), rw_pad, rw_t)
    xl = _moe(hpk, afft, xl, mod_rows(i, 5), expert_w_gate, expert_w_up, expert_w_down, i, None)

    i = 1
    groups = fourier_w.shape[1]
    gw = fourier_w.shape[2]
    cc, sc_ = _dft_tables(gw)
    cs_c = jnp.concatenate([cc, sc_], axis=1).astype(BF)
    cn, sn = _dft_tables(n)
    h = _normmod(xl, norm_mix[i][None, :], mod_rows(i, 1), mod_rows(i, 0))
    pool_width = pool_w.shape[1] * pool_w.shape[2]
    w_odd = odd_w_in[0].astype(BF)
    zp, uv = _odd_in(h, w_odd[:, :pool_width], w_odd[:, pool_width:], cs_c, groups)
    yp = _pool_mix(zp, pool_w[0].astype(BF), pool_scale[0][None, :])
    yf = _fourier_mix(uv, cn.astype(BF), sn.astype(BF), fourier_w[0].astype(BF))
    rw_pad, rw_t = router_mats(i)
    xl, hpk, afft = _mixer_out(yp, yf, odd_w_out[0].astype(BF), xl, mod_rows(i, 2), norm_ffn[i][None, :],
                               mod_rows(i, 4), mod_rows(i, 3), rw_pad, rw_t)
    return _moe(hpk, afft, xl, mod_rows(i, 5), expert_w_gate, expert_w_up, expert_w_down, i,
                final_norm[None, :])
```

```python
import functools
import math

import jax
import jax.numpy as jnp
from jax import lax
from jax.experimental import pallas as pl
from jax.experimental.pallas import tpu as pltpu

F32 = jnp.float32
BF = jnp.bfloat16
I32 = jnp.int32
U32 = jnp.uint32

GRID_W = 64
ROPE_BASE = 10000.0
EPS = 1e-6
MLA_HEADS = 8
MLA_Q_LORA = 512
MLA_KV_LORA = 512
MLA_NOPE = 128
MLA_ROPE = 64
MLA_V = 128
DIFF_HEADS = 4
DIFF_DIM = 128
DIFF_QK = 2 * DIFF_HEADS * DIFF_DIM
POOL_WINDOWS = (2, 4, 8, 16)
N_EXPERTS = 16
EC_CAPACITY = 2

LANE = 128
SUBLANE_BF16 = 16
VMEM_LIMIT = 56 * 1024 * 1024

TOKEN_BLOCK = 256
CHUNK_ROWS = 64
COUNT_CHUNK = 128
AFF_W = LANE


def _cparams(*sem):
    return pltpu.CompilerParams(dimension_semantics=sem, vmem_limit_bytes=VMEM_LIMIT)


def _tile(total, target, mult=SUBLANE_BF16):
    best = None
    for t in range(mult, min(total, target) + 1, mult):
        if total % t == 0:
            best = t
    assert best is not None, (total, target)
    return best


def _const_spec(shape):
    nd = len(shape)
    return pl.BlockSpec(shape, lambda *_: (0,) * nd, pipeline_mode=pl.Buffered(1))


def _mod_kernel(c_ref, w_ref, b_ref, o_ref):
    cv = c_ref[...]
    s = (cv * jax.nn.sigmoid(cv)).astype(BF)
    o_ref[0] = jnp.dot(s, w_ref[0].astype(BF), preferred_element_type=F32) + b_ref[0]


def _modulation(c8, mod_w, mod_b):
    depth, d, n6 = mod_w.shape
    tn = 1024
    return pl.pallas_call(
        _mod_kernel,
        grid=(depth, n6 // tn),
        in_specs=[
            pl.BlockSpec((8, d), lambda i, j: (0, 0)),
            pl.BlockSpec((1, d, tn), lambda i, j: (i, 0, j)),
            pl.BlockSpec((1, 1, tn), lambda i, j: (i, 0, j)),
        ],
        out_specs=pl.BlockSpec((1, 8, tn), lambda i, j: (i, 0, j)),
        out_shape=jax.ShapeDtypeStruct((depth, 8, n6), F32),
        compiler_params=_cparams("parallel", "parallel"),
        name="modulation",
    )(c8, mod_w, mod_b.reshape(depth, 1, n6))


def _norm_mod(xf, gain, scale, shift):
    ms = jnp.mean(xf * xf, axis=-1, keepdims=True)
    y = xf * lax.rsqrt(ms + EPS) * gain
    return y * (1.0 + scale) + shift


def _normmod_cat_kernel(x_ref, ctx_ref, g_ref, sc_ref, sh_ref, scc_ref, shc_ref, o_ref, *, n_lat_blocks):
    i = pl.program_id(1)

    @pl.when(i < n_lat_blocks)
    def _():
        o_ref[0] = _norm_mod(x_ref[0], g_ref[...], sc_ref[0], sh_ref[0]).astype(BF)

    @pl.when(i >= n_lat_blocks)
    def _():
        o_ref[0] = _norm_mod(ctx_ref[0], g_ref[...], scc_ref[...], shc_ref[...]).astype(BF)


def _normmod_cat(x, ctx, gain, sc, sh, scc, shc):
    b, n, d = x.shape
    n_ctx = ctx.shape[1]
    tm = math.gcd(n, n_ctx)
    tm = _tile(tm, 512)
    nl, nc = n // tm, n_ctx // tm
    return pl.pallas_call(
        functools.partial(_normmod_cat_kernel, n_lat_blocks=nl),
        grid=(b, nl + nc),
        in_specs=[
            pl.BlockSpec((1, tm, d), lambda bi, i: (bi, jnp.minimum(i, nl - 1), 0)),
            pl.BlockSpec((1, tm, d), lambda bi, i: (bi, jnp.maximum(i - nl, 0), 0)),
            pl.BlockSpec((1, d), lambda bi, i: (0, 0)),
            pl.BlockSpec((1, 1, d), lambda bi, i: (bi, 0, 0)),
            pl.BlockSpec((1, 1, d), lambda bi, i: (bi, 0, 0)),
            pl.BlockSpec((1, d), lambda bi, i: (0, 0)),
            pl.BlockSpec((1, d), lambda bi, i: (0, 0)),
        ],
        out_specs=pl.BlockSpec((1, tm, d), lambda bi, i: (bi, i, 0)),
        out_shape=jax.ShapeDtypeStruct((b, n + n_ctx, d), BF),
        compiler_params=_cparams("parallel", "parallel"),
        name="normmod_cat",
    )(x, ctx, gain, sc, sh, scc, shc)


def _normmod_kernel(x_ref, g_ref, sc_ref, sh_ref, o_ref):
    o_ref[0] = _norm_mod(x_ref[0], g_ref[...], sc_ref[0], sh_ref[0]).astype(BF)


def _normmod(x, gain, sc, sh):
    b, n, d = x.shape
    tm = _tile(n, 512)
    return pl.pallas_call(
        _normmod_kernel,
        grid=(b, n // tm),
        in_specs=[
            pl.BlockSpec((1, tm, d), lambda bi, i: (bi, i, 0)),
            pl.BlockSpec((1, d), lambda bi, i: (0, 0)),
            pl.BlockSpec((1, 1, d), lambda bi, i: (bi, 0, 0)),
            pl.BlockSpec((1, 1, d), lambda bi, i: (bi, 0, 0)),
        ],
        out_specs=pl.BlockSpec((1, tm, d), lambda bi, i: (bi, i, 0)),
        out_shape=jax.ShapeDtypeStruct((b, n, d), BF),
        compiler_params=_cparams("parallel", "parallel"),
        name="normmod",
    )(x, gain, sc, sh)


def _rope(xb, cs, sn):
    return xb * cs + pltpu.roll(xb, LANE // 2, 1) * sn


def _rope_tables(n, n_ctx, quarter, pad):
    t = jnp.arange(n)
    row = (t // GRID_W).astype(F32)
    col = (t % GRID_W).astype(F32)
    inv_freq = ROPE_BASE ** (-jnp.arange(quarter, dtype=F32) / quarter)
    ang = jnp.concatenate([row[:, None] * inv_freq, col[:, None] * inv_freq], axis=1)
    cos, sin = jnp.cos(ang), jnp.sin(ang)
    zpad = jnp.zeros((n, pad), F32)
    cs = jnp.concatenate([cos, zpad, cos, zpad], axis=1)
    sn = jnp.concatenate([-sin, zpad, sin, zpad], axis=1)
    cs = jnp.concatenate([cs, jnp.ones((n_ctx, LANE), F32)], axis=0)
    sn = jnp.concatenate([sn, jnp.zeros((n_ctx, LANE), F32)], axis=0)
    return cs, sn


def _rope_perm(quarter, pad):
    src = []
    for j in range(2):
        for a in range(2):
            for f in range(quarter):
                src.append(a * 2 * quarter + j * quarter + f)
        src.extend([-1] * pad)
    return src


def _take_cols(w, src):
    wz = jnp.concatenate([w, jnp.zeros((w.shape[0], 1), w.dtype)], axis=1)
    idx = jnp.array([s if s >= 0 else w.shape[1] for s in src], I32)
    return jnp.take(wz, idx, axis=1)


def _rms_gain(v, g):
    ms = jnp.mean(v * v, axis=-1, keepdims=True)
    return v * lax.rsqrt(ms + EPS) * g


def _proj_lora_kernel(a_ref, w_ref, gq_ref, gkv_ref, cs_ref, sn_ref, cq_ref, ckv_ref, kr_ref):
    acc = jnp.dot(a_ref[0], w_ref[...], preferred_element_type=F32)
    ql, kvl = MLA_Q_LORA, MLA_KV_LORA
    cq_ref[0] = _rms_gain(acc[:, :ql], gq_ref[...]).astype(BF)
    ckv_ref[0] = _rms_gain(acc[:, ql:ql + kvl], gkv_ref[...]).astype(BF)
    kr_ref[0] = _rope(acc[:, ql + kvl:], cs_ref[...], sn_ref[...]).astype(BF)


def _proj_lora(h, w, gq, gkv, cs, sn):
    b, nt, d = h.shape
    tm = _tile(nt, 1100)
    nw = w.shape[1]
    return pl.pallas_call(
        _proj_lora_kernel,
        grid=(b, nt // tm),
        in_specs=[
            pl.BlockSpec((1, tm, d), lambda bi, i: (bi, i, 0)),
            _const_spec((d, nw)),
            _const_spec((1, MLA_Q_LORA)),
            _const_spec((1, MLA_KV_LORA)),
            pl.BlockSpec((tm, LANE), lambda bi, i: (i, 0)),
            pl.BlockSpec((tm, LANE), lambda bi, i: (i, 0)),
        ],
        out_specs=[
            pl.BlockSpec((1, tm, MLA_Q_LORA), lambda bi, i: (bi, i, 0)),
            pl.BlockSpec((1, tm, MLA_KV_LORA), lambda bi, i: (bi, i, 0)),
            pl.BlockSpec((1, tm, LANE), lambda bi, i: (bi, i, 0)),
        ],
        out_shape=[
            jax.ShapeDtypeStruct((b, nt, MLA_Q_LORA), BF),
            jax.ShapeDtypeStruct((b, nt, MLA_KV_LORA), BF),
            jax.ShapeDtypeStruct((b, nt, LANE), BF),
        ],
        compiler_params=_cparams("parallel", "parallel"),
        name="proj_lora",
    )(h, w, gq, gkv, cs, sn)


def _proj_diff_kernel(a_ref, w_ref, cs_ref, sn_ref, o_ref, *, n_rope_tiles):
    j = pl.program_id(0)
    acc = jnp.dot(a_ref[0], w_ref[...], preferred_element_type=F32)
    tn = acc.shape[1]

    @pl.when(j < n_rope_tiles)
    def _():
        cs, sn = cs_ref[...], sn_ref[...]
        for blk in range(tn // LANE):
            sl = slice(blk * LANE, (blk + 1) * LANE)
            o_ref[0, :, sl] = _rope(acc[:, sl], cs, sn).astype(BF)

    @pl.when(j >= n_rope_tiles)
    def _():
        o_ref[0] = acc.astype(BF)


def _proj_diff(h, w, cs, sn):
    b, nt, d = h.shape
    tm = _tile(nt, 1100)
    tn = 1024
    nw = w.shape[1]
    return pl.pallas_call(
        functools.partial(_proj_diff_kernel, n_rope_tiles=2 * DIFF_QK // tn),
        grid=(nw // tn, b, nt // tm),
        in_specs=[
            pl.BlockSpec((1, tm, d), lambda j, bi, i: (bi, i, 0)),
            pl.BlockSpec((d, tn), lambda j, bi, i: (0, j)),
            pl.BlockSpec((tm, LANE), lambda j, bi, i: (i, 0)),
            pl.BlockSpec((tm, LANE), lambda j, bi, i: (i, 0)),
        ],
        out_specs=pl.BlockSpec((1, tm, tn), lambda j, bi, i: (bi, i, j)),
        out_shape=jax.ShapeDtypeStruct((b, nt, nw), BF),
        compiler_params=_cparams("parallel", "parallel", "parallel"),
        name="proj_diff",
    )(h, w, cs, sn)


def _up_q_kernel(a_ref, w_ref, cs_ref, sn_ref, o_ref):
    acc = jnp.dot(a_ref[0], w_ref[...], preferred_element_type=F32)
    cs, sn = cs_ref[...], sn_ref[...]
    for hd in range(MLA_HEADS):
        c0 = hd * 2 * LANE
        o_ref[0, :, c0:c0 + LANE] = acc[:, c0:c0 + LANE].astype(BF)
        o_ref[0, :, c0 + LANE:c0 + 2 * LANE] = _rope(acc[:, c0 + LANE:c0 + 2 * LANE], cs, sn).astype(BF)


def _up_q(cq, w, cs, sn, n):
    b, nt, k = cq.shape
    tm = _tile(n, 1024)
    nw = w.shape[1]
    return pl.pallas_call(
        _up_q_kernel,
        grid=(b, n // tm),
        in_specs=[
            pl.BlockSpec((1, tm, k), lambda bi, i: (bi, i, 0)),
            _const_spec((k, nw)),
            pl.BlockSpec((tm, LANE), lambda bi, i: (i, 0)),
            pl.BlockSpec((tm, LANE), lambda bi, i: (i, 0)),
        ],
        out_specs=pl.BlockSpec((1, tm, nw), lambda bi, i: (bi, i, 0)),
        out_shape=jax.ShapeDtypeStruct((b, n, nw), BF),
        compiler_params=_cparams("parallel", "parallel"),
        name="up_q",
    )(cq, w, cs, sn)


def _up_kv_kernel(a_ref, w_ref, kr_ref, k_ref, v_ref):
    acc = jnp.dot(a_ref[0], w_ref[...], preferred_element_type=F32)
    kr = kr_ref[0]
    for hd in range(MLA_HEADS):
        c0 = hd * 2 * LANE
        k_ref[0, :, c0:c0 + LANE] = acc[:, c0:c0 + LANE].astype(BF)
        k_ref[0, :, c0 + LANE:c0 + 2 * LANE] = kr
        v_ref[0, :, hd * LANE:(hd + 1) * LANE] = acc[:, c0 + LANE:c0 + 2 * LANE].astype(BF)


def _up_kv(ckv, w, kr):
    b, nt, k = ckv.shape
    tm = _tile(nt, 1100)
    nw = w.shape[1]
    return pl.pallas_call(
        _up_kv_kernel,
        grid=(b, nt // tm),
        in_specs=[
            pl.BlockSpec((1, tm, k), lambda bi, i: (bi, i, 0)),
            _const_spec((k, nw)),
            pl.BlockSpec((1, tm, LANE), lambda bi, i: (bi, i, 0)),
        ],
        out_specs=[
            pl.BlockSpec((1, tm, nw), lambda bi, i: (bi, i, 0)),
            pl.BlockSpec((1, tm, MLA_HEADS * MLA_V), lambda bi, i: (bi, i, 0)),
        ],
        out_shape=[
            jax.ShapeDtypeStruct((b, nt, nw), BF),
            jax.ShapeDtypeStruct((b, nt, MLA_HEADS * MLA_V), BF),
        ],
        compiler_params=_cparams("parallel", "parallel"),
        name="up_kv",
    )(ckv, w, kr)


def _softmax_attend(qs, load_k, load_v, nk, dv, scale, kv_chunk):
    tq = qs[0].shape[0]
    c_exp = scale * math.log2(math.e)

    def step(c0, size, carry):
        out = []
        for i, (q, (m, l, acc)) in enumerate(zip(qs, carry)):
            s = lax.dot_general(q, load_k(i, c0, size), (((1,), (1,)), ((), ())),
                                preferred_element_type=F32)
            m_new = jnp.maximum(m, jnp.max(s, axis=-1, keepdims=True))
            alpha = jnp.exp2((m - m_new) * c_exp)
            p = jnp.exp2((s - m_new) * c_exp)
            l = alpha * l + jnp.sum(p, axis=-1, keepdims=True)
            acc = alpha * acc + jnp.dot(p.astype(BF), load_v(i, c0, size), preferred_element_type=F32)
            out.append((m_new, l, acc))
        return tuple(out)

    carry = tuple((jnp.full((tq, 1), -jnp.inf, F32), jnp.zeros((tq, 1), F32), jnp.zeros((tq, dv), F32))
                  for _ in qs)
    for c0 in range(0, nk, kv_chunk):
        carry = step(c0, min(kv_chunk, nk - c0), carry)
    return [acc / l for (_, l, acc) in carry]


def _mla_attn_kernel(q_ref, k_ref, v_ref, o_ref, *, scale, kv_chunk, heads):
    dk, dv = q_ref.shape[2] // heads, v_ref.shape[2] // heads
    qs = [q_ref[0, :, h * dk:(h + 1) * dk] for h in range(heads)]
    outs = _softmax_attend(
        qs,
        lambda h, c0, size: k_ref[0, pl.ds(c0, size), h * dk:(h + 1) * dk],
        lambda h, c0, size: v_ref[0, pl.ds(c0, size), h * dv:(h + 1) * dv],
        k_ref.shape[1], dv, scale, kv_chunk)
    for h in range(heads):
        o_ref[0, :, h * dv:(h + 1) * dv] = outs[h].astype(BF)


def _mla_attention(q, k, v, n):
    b, _, _ = q.shape
    nt = k.shape[1]
    tq = _tile(n, 512)
    dk = 2 * LANE
    heads = 2
    return pl.pallas_call(
        functools.partial(_mla_attn_kernel, scale=1.0 / math.sqrt(MLA_NOPE + MLA_ROPE),
                          kv_chunk=min(1024, n), heads=heads),
        grid=(b, MLA_HEADS // heads, n // tq),
        in_specs=[
            pl.BlockSpec((1, tq, heads * dk), lambda bi, h, i: (bi, i, h)),
            pl.BlockSpec((1, nt, heads * dk), lambda bi, h, i: (bi, 0, h)),
            pl.BlockSpec((1, nt, heads * MLA_V), lambda bi, h, i: (bi, 0, h)),
        ],
        out_specs=pl.BlockSpec((1, tq, heads * MLA_V), lambda bi, h, i: (bi, i, h)),
        out_shape=jax.ShapeDtypeStruct((b, n, MLA_HEADS * MLA_V), BF),
        compiler_params=_cparams("parallel", "parallel", "parallel"),
        name="mla_attention",
    )(q, k, v)


def _diff_attn_kernel(q1_ref, q2_ref, k1_ref, k2_ref, v_ref, lq1_ref, lk1_ref, lq2_ref, lk2_ref,
                      g_ref, o_ref, *, scale, lam_init, kv_chunk):
    k_refs = (k1_ref, k2_ref)
    o1, o2 = _softmax_attend(
        [q1_ref[0], q2_ref[0]],
        lambda i, c0, size: k_refs[i][0, pl.ds(c0, size), :],
        lambda i, c0, size: v_ref[0, pl.ds(c0, size), :],
        v_ref.shape[1], v_ref.shape[2], scale, kv_chunk)
    lam = (jnp.exp(jnp.sum(lq1_ref[...] * lk1_ref[...], axis=-1, keepdims=True))
           - jnp.exp(jnp.sum(lq2_ref[...] * lk2_ref[...], axis=-1, keepdims=True)) + lam_init)
    od = o1 - lam * o2
    o_ref[0] = (_rms_gain(od, g_ref[...]) * (1.0 - lam_init)).astype(BF)


def _diff_attention(qkv, lq1, lk1, lq2, lk2, subln, n, layer_idx):
    b, nt, _ = qkv.shape
    tq = _tile(n, 512)
    hq = DIFF_HEADS
    dv = 2 * DIFF_DIM
    lam_init = 0.8 - 0.6 * math.exp(-0.3 * layer_idx)
    v_block0 = 2 * DIFF_QK // dv
    vec = lambda: _const_spec((1, DIFF_DIM))
    return pl.pallas_call(
        functools.partial(_diff_attn_kernel, scale=1.0 / math.sqrt(DIFF_DIM), lam_init=lam_init,
                          kv_chunk=min(1024, n)),
        grid=(b, hq, n // tq),
        in_specs=[
            pl.BlockSpec((1, tq, DIFF_DIM), lambda bi, h, i: (bi, i, h)),
            pl.BlockSpec((1, tq, DIFF_DIM), lambda bi, h, i: (bi, i, hq + h)),
            pl.BlockSpec((1, nt, DIFF_DIM), lambda bi, h, i: (bi, 0, 2 * hq + h)),
            pl.BlockSpec((1, nt, DIFF_DIM), lambda bi, h, i: (bi, 0, 3 * hq + h)),
            pl.BlockSpec((1, nt, dv), lambda bi, h, i: (bi, 0, v_block0 + h)),
            vec(), vec(), vec(), vec(),
            _const_spec((1, dv)),
        ],
        out_specs=pl.BlockSpec((1, tq, dv), lambda bi, h, i: (bi, i, h)),
        out_shape=jax.ShapeDtypeStruct((b, n, hq * dv), BF),
        compiler_params=_cparams("parallel", "parallel", "parallel"),
        name="diff_attention",
    )(qkv, qkv, qkv, qkv, qkv, lq1, lk1, lq2, lk2, subln)


def _mixer_out_kernel(a1_ref, a2_ref, w_ref, x_ref, g1_ref, gn_ref, sc_ref, sh_ref, rw_ref, rwt_ref,
                      xo_ref, hpk_ref, afft_ref):
    k1 = a1_ref.shape[2]
    y = (jnp.dot(a1_ref[0], w_ref[:k1, :], preferred_element_type=F32)
         + jnp.dot(a2_ref[0], w_ref[k1:, :], preferred_element_type=F32))
    xl = x_ref[0] + g1_ref[0] * y
    xo_ref[0] = xl
    hb = _norm_mod(xl, gn_ref[...], sc_ref[0], sh_ref[0]).astype(BF)
    d = hb.shape[1]
    lane = lax.broadcasted_iota(I32, (1, LANE), 1)
    lg = jnp.dot(hb, rw_ref[...], preferred_element_type=F32)
    lg = jnp.where(lane < N_EXPERTS, lg, -jnp.inf)
    ex = jnp.exp(lg - jnp.max(lg, axis=-1, keepdims=True))
    aff = ex / jnp.sum(ex, axis=-1, keepdims=True)
    lgt = lax.dot_general(rwt_ref[...], hb, (((1,), (1,)), ((), ())), preferred_element_type=F32)
    ext = jnp.exp(lgt - jnp.max(lgt, axis=0, keepdims=True))
    afft_ref[0] = ext / jnp.sum(ext, axis=0, keepdims=True)
    hpk_ref[0, :, :d] = hb.astype(F32)
    hpk_ref[0, :, d:] = aff


def _mixer_out(a1, a2, w, x, g1, gn, sc, sh, rw, rwt):
    b, n, d = x.shape
    k1, k2 = a1.shape[2], a2.shape[2]
    tm = _tile(n, 512)
    return pl.pallas_call(
        _mixer_out_kernel,
        grid=(b, n // tm),
        in_specs=[
            pl.BlockSpec((1, tm, k1), lambda bi, i: (bi, i, 0)),
            pl.BlockSpec((1, tm, k2), lambda bi, i: (bi, i, 0)),
            _const_spec((k1 + k2, d)),
            pl.BlockSpec((1, tm, d), lambda bi, i: (bi, i, 0)),
            pl.BlockSpec((1, 1, d), lambda bi, i: (bi, 0, 0)),
            _const_spec((1, d)),
            pl.BlockSpec((1, 1, d), lambda bi, i: (bi, 0, 0)),
            pl.BlockSpec((1, 1, d), lambda bi, i: (bi, 0, 0)),
            _const_spec((d, LANE)),
            _const_spec((N_EXPERTS, d)),
        ],
        out_specs=[
            pl.BlockSpec((1, tm, d), lambda bi, i: (bi, i, 0)),
            pl.BlockSpec((1, tm, d + AFF_W), lambda bi, i: (bi, i, 0)),
            pl.BlockSpec((1, N_EXPERTS, tm), lambda bi, i: (bi, 0, i)),
        ],
        out_shape=[
            jax.ShapeDtypeStruct((b, n, d), F32),
            jax.ShapeDtypeStruct((b, n, d + AFF_W), F32),
            jax.ShapeDtypeStruct((b, N_EXPERTS, n), F32),
        ],
        compiler_params=_cparams("parallel", "parallel"),
        name="mixer_out",
    )(a1, a2, w, x, g1, gn, sc, sh, rw, rwt)


def _route_kernel(afft_ref, tv_ref, tri_ref, idx_ref, pos_ref, offs_ref, cnt_scr, *, cap):
    a = afft_ref[0]
    ne, n = a.shape

    def count_ge(th):
        return jnp.sum(jnp.where(a >= th, 1.0, 0.0), axis=1, keepdims=True)

    def bracket_values(lo, hi):
        vmin = jnp.min(jnp.where(a >= lo, a, jnp.inf), axis=1, keepdims=True)
        vmax = jnp.max(jnp.where(a < hi, a, -jnp.inf), axis=1, keepdims=True)
        return vmin, vmax

    def unresolved(state):
        vmin, vmax = bracket_values(*state)
        return jnp.max(jnp.where(vmin < vmax, 1.0, 0.0)) > 0.5

    def bisect(state):
        lo, hi = state
        mid = 0.5 * (lo + hi)
        keep = count_ge(mid) >= cap
        return jnp.where(keep, mid, lo), jnp.where(keep, hi, mid)

    lo, hi = lax.while_loop(unresolved, bisect, (jnp.zeros((ne, 1), F32), jnp.full((ne, 1), 2.0, F32)))
    thr, _ = bracket_values(lo, hi)
    gt = a > thr
    eq = a == thr
    need = cap - jnp.sum(jnp.where(gt, 1.0, 0.0), axis=1, keepdims=True)

    def prefix_count(mask, emit_offsets):
        ones = jnp.where(mask, 1.0, 0.0)
        off = jnp.zeros((ne, 1), F32)
        lane = lax.broadcasted_iota(I32, (1, LANE), 1)
        offs = jnp.zeros((ne, LANE), F32)
        for c in range(n // COUNT_CHUNK):
            sl = slice(c * COUNT_CHUNK, (c + 1) * COUNT_CHUNK)
            if emit_offsets:
                offs = jnp.where(lane == c, off, offs)
            cs = jnp.dot(ones[:, sl].astype(BF), tri_ref[...], preferred_element_type=F32) + off
            cnt_scr[:, sl] = cs
            off = cs[:, COUNT_CHUNK - 1:COUNT_CHUNK]
        if emit_offsets:
            offs = jnp.where(lane == n // COUNT_CHUNK, off, offs)
        return cnt_scr[...], offs

    cum_eq, _ = prefix_count(eq, False)
    sel = gt | (eq & (cum_eq <= need))
    cum_sel, offs = prefix_count(sel, True)
    pos = jnp.where(sel, cum_sel.astype(I32) - 1, -1)
    pos_ref[0] = pos
    offs_ref[0] = offs.astype(I32)
    slot = lax.broadcasted_iota(I32, (cap, n), 0)
    for e in range(ne):
        onehot = jnp.where(pos[e:e + 1, :] == slot, 1.0, 0.0).astype(BF)
        r = lax.dot_general(tv_ref[...], onehot, (((1,), (1,)), ((), ())), preferred_element_type=F32)
        idx_ref[0, e:e + 1, :] = (r[0:1, :] * 64.0 + r[1:2, :]).astype(I32)


def _route(afft):
    b, ne, n = afft.shape
    cap = (EC_CAPACITY * n) // N_EXPERTS
    assert n % COUNT_CHUNK == 0 and n // COUNT_CHUNK < LANE and n <= 64 * 256
    t = jnp.arange(n)
    tv = jnp.zeros((8, n), F32).at[0].set((t // 64).astype(F32)).at[1].set((t % 64).astype(F32)).astype(BF)
    tri = (jnp.arange(COUNT_CHUNK)[:, None] <= jnp.arange(COUNT_CHUNK)[None, :]).astype(BF)
    return pl.pallas_call(
        functools.partial(_route_kernel, cap=cap),
        grid=(b,),
        in_specs=[
            pl.BlockSpec((1, ne, n), lambda bi: (bi, 0, 0)),
            pl.BlockSpec((8, n), lambda bi: (0, 0)),
            pl.BlockSpec((COUNT_CHUNK, COUNT_CHUNK), lambda bi: (0, 0)),
        ],
        out_specs=[
            pl.BlockSpec((1, ne, cap), lambda bi: (bi, 0, 0)),
            pl.BlockSpec((1, ne, n), lambda bi: (bi, 0, 0)),
            pl.BlockSpec((1, ne, LANE), lambda bi: (bi, 0, 0)),
        ],
        out_shape=[
            jax.ShapeDtypeStruct((b, ne, cap), I32),
            jax.ShapeDtypeStruct((b, ne, n), I32),
            jax.ShapeDtypeStruct((b, ne, LANE), I32),
        ],
        scratch_shapes=[pltpu.VMEM((ne, n), F32)],
        compiler_params=_cparams("parallel"),
        name="route",
    )(afft, tv, tri)


def _ffn_kernel(idx_ref, h_hbm, wg_ref, wu_ref, wd_ref, y_ref, land, xe, hid, gate, sem,
                *, m, row_chunk, n_up):
    e = pl.program_id(0)
    s = pl.program_id(2)
    tile = e * pl.num_programs(1) + pl.program_id(1)
    n_tiles = pl.num_programs(0) * pl.num_programs(1)
    d = wg_ref.shape[2]

    def row_copy(t, j):
        row = idx_ref[t * m + j]
        return pltpu.make_async_copy(h_hbm.at[pl.ds(row, 1), :], land.at[pl.ds(j, 1), :], sem)

    def issue_gather(t):
        def body(j, c):
            row_copy(t, j).start()
            return c
        lax.fori_loop(0, m, body, 0, unroll=8)

    @pl.when(s == 0)
    def _():
        @pl.when(tile == 0)
        def _():
            issue_gather(tile)

        def drain(j, c):
            row_copy(tile, j).wait()
            return c
        lax.fori_loop(0, m, drain, 0, unroll=8)
        lane = lax.broadcasted_iota(I32, (1, LANE), 1)
        for rc in range(m // row_chunk):
            rows = slice(rc * row_chunk, (rc + 1) * row_chunk)
            xe[rows, :] = land[rows, :d].astype(BF)
            gate[rows, :] = jnp.sum(jnp.where(lane == e, land[rows, d:], 0.0), axis=1, keepdims=True)

        @pl.when(tile + 1 < n_tiles)
        def _():
            issue_gather(tile + 1)

    @pl.when(s < n_up)
    def _():
        wg = wg_ref[0, 0].astype(BF)
        wu = wu_ref[0, 0].astype(BF)
        for rc in range(m // row_chunk):
            rows = slice(rc * row_chunk, (rc + 1) * row_chunk)
            x = xe[rows, :]
            g = jnp.dot(x, wg, preferred_element_type=F32)
            u = jnp.dot(x, wu, preferred_element_type=F32)
            hid[s, rows, :] = (g * jax.nn.sigmoid(g) * u).astype(BF)

    @pl.when(s >= n_up)
    def _():
        wd = wd_ref[0, 0].astype(BF)
        tf = hid.shape[2]
        for rc in range(m // row_chunk):
            rows = slice(rc * row_chunk, (rc + 1) * row_chunk)
            acc = jnp.dot(hid[0, rows, :], wd[:tf], preferred_element_type=F32)
            for f in range(1, n_up):
                acc = acc + jnp.dot(hid[f, rows, :], wd[f * tf:(f + 1) * tf], preferred_element_type=F32)
            y_ref[0, rows, :] = (acc * gate[rows, :]).astype(BF)


def _expert_ffn(idx_flat, h_rows, w_gate, w_up, w_down, layer, m_total):
    _, ne, d, ff = w_gate.shape
    n_half = 2
    m = m_total // n_half
    tf = _tile(ff, 512, LANE)
    tn = _tile(d, 512, LANE)
    n_up, n_down = ff // tf, d // tn
    row_chunk = _tile(m, 512)
    up_spec = pl.BlockSpec((1, 1, d, tf), lambda e, hf, s, idx: (layer, e, 0, jnp.minimum(s, n_up - 1)))
    return pl.pallas_call(
        functools.partial(_ffn_kernel, m=m, row_chunk=row_chunk, n_up=n_up),
        grid_spec=pltpu.PrefetchScalarGridSpec(
            num_scalar_prefetch=1,
            grid=(ne, n_half, n_up + n_down),
            in_specs=[
                pl.BlockSpec(memory_space=pl.ANY),
                up_spec,
                up_spec,
                pl.BlockSpec((1, 1, ff, tn), lambda e, hf, s, idx: (layer, e, 0, jnp.maximum(s - n_up, 0))),
            ],
            out_specs=pl.BlockSpec((1, m, tn), lambda e, hf, s, idx: (e, hf, jnp.maximum(s - n_up, 0))),
            scratch_shapes=[
                pltpu.VMEM((m, d + AFF_W), F32),
                pltpu.VMEM((m, d), BF),
                pltpu.VMEM((n_up, m, tf), BF),
                pltpu.VMEM((m, 1), F32),
                pltpu.SemaphoreType.DMA(()),
            ],
        ),
        out_shape=jax.ShapeDtypeStruct((ne, m_total, d), BF),
        compiler_params=_cparams("arbitrary", "arbitrary", "arbitrary"),
        name="expert_ffn",
    )(idx_flat, h_rows, w_gate, w_up, w_down)


def _combine_kernel(meta_ref, y_hbm, pos_ref, x_ref, g_ref, *rest, cap, nblk, mode):
    if mode == "final":
        gn_ref, o_ref, ybuf, sems = rest
    else:
        o_ref, ybuf, sems = rest
    step = pl.program_id(0) * nblk + pl.program_id(1)
    n_steps = pl.num_programs(0) * nblk
    slot = step % 2
    pos = pos_ref[0]
    tm, ne = pos.shape
    lane = lax.broadcasted_iota(I32, (1, LANE), 1)
    upper = lane >= CHUNK_ROWS
    lane_row = lane % CHUNK_ROWS

    def plan(st, r):
        first = [meta_ref[st * 32 + 1 + e] + r * CHUNK_ROWS for e in range(ne)]
        start = [jnp.minimum(fs, cap - CHUNK_ROWS) for fs in first]
        return first, start

    def chunk_copy(st, buf, e, start):
        row0 = pl.multiple_of((st // nblk) * cap + start, SUBLANE_BF16)
        return pltpu.make_async_copy(
            y_hbm.at[e, pl.ds(row0, CHUNK_ROWS), :],
            ybuf.at[buf, pl.ds(e * CHUNK_ROWS, CHUNK_ROWS), :], sems.at[buf, e])

    def fetch(st, buf, r):
        _, start = plan(st, r)
        for e in range(ne):
            chunk_copy(st, buf, e, start[e]).start()

    def place(buf, r):
        first, start = plan(step, r)
        pieces = []
        for p in range(ne // 2):
            e0, e1 = 2 * p, 2 * p + 1
            pe = jnp.where(upper, pos[:, e1:e1 + 1], pos[:, e0:e0 + 1])
            st = jnp.where(upper, start[e1], start[e0])
            fs = jnp.where(upper, first[e1], first[e0])
            hit = (pe - st == lane_row) & (pe >= fs)
            pieces.append(jnp.where(hit, 1.0, 0.0).astype(BF))
        onehot = jnp.concatenate(pieces, axis=1)
        for e in range(ne):
            chunk_copy(step, buf, e, start[e]).wait()
        return jnp.dot(onehot, ybuf[buf], preferred_element_type=F32)

    @pl.when(step == 0)
    def _():
        fetch(step, slot, 0)

    @pl.when(step + 1 < n_steps)
    def _():
        fetch(step + 1, 1 - slot, 0)

    moe = place(slot, 0)

    def extra_round(r, acc):
        fetch(step, slot, r)
        return acc + place(slot, r)

    moe = lax.fori_loop(1, meta_ref[step * 32], extra_round, moe)
    xl = x_ref[0] + g_ref[0] * moe
    if mode == "final":
        o_ref[0] = _rms_gain(xl, gn_ref[...])
    else:
        o_ref[0] = xl


def _combine(meta, y, pos_t, x, g2, final_gain):
    b, n, d = x.shape
    ne = y.shape[0]
    cap = y.shape[1] // b
    nblk = n // TOKEN_BLOCK
    assert cap >= CHUNK_ROWS and cap % SUBLANE_BF16 == 0 and 2 * CHUNK_ROWS == LANE
    mode = "final" if final_gain is not None else "plain"
    in_specs = [
        pl.BlockSpec(memory_space=pl.ANY),
        pl.BlockSpec((1, TOKEN_BLOCK, ne), lambda bi, i, meta: (bi, i, 0)),
        pl.BlockSpec((1, TOKEN_BLOCK, d), lambda bi, i, meta: (bi, i, 0)),
        pl.BlockSpec((1, 1, d), lambda bi, i, meta: (bi, 0, 0)),
    ]
    args = [meta, y, pos_t, x, g2]
    if mode == "final":
        in_specs.append(pl.BlockSpec((1, d), lambda bi, i, meta: (0, 0)))
        args.append(final_gain)
    return pl.pallas_call(
        functools.partial(_combine_kernel, cap=cap, nblk=nblk, mode=mode),
        grid_spec=pltpu.PrefetchScalarGridSpec(
            num_scalar_prefetch=1,
            grid=(b, nblk),
            in_specs=in_specs,
            out_specs=pl.BlockSpec((1, TOKEN_BLOCK, d), lambda bi, i, meta: (bi, i, 0)),
            scratch_shapes=[
                pltpu.VMEM((2, ne * CHUNK_ROWS, d), BF),
                pltpu.SemaphoreType.DMA((2, ne)),
            ],
        ),
        out_shape=jax.ShapeDtypeStruct((b, n, d), F32),
        compiler_params=_cparams("arbitrary", "arbitrary"),
        name="moe_combine",
    )(*args)


def _moe(hpk, afft, xl, g2, w_gate, w_up, w_down, layer, final_gain):
    b, n, d = xl.shape
    ne = N_EXPERTS
    idx, pos, offs = _route(afft)
    cap = idx.shape[2]
    rows = idx + (jnp.arange(b, dtype=I32) * n)[:, None, None]
    idx_flat = jnp.transpose(rows, (1, 0, 2)).reshape(-1)
    per_blk = TOKEN_BLOCK // COUNT_CHUNK
    nblk = n // TOKEN_BLOCK
    lo = offs[:, :, 0:nblk * per_blk:per_blk]
    hi = offs[:, :, per_blk:nblk * per_blk + 1:per_blk]
    first = (lo // SUBLANE_BF16) * SUBLANE_BF16
    rounds = jnp.max((hi - first + CHUNK_ROWS - 1) // CHUNK_ROWS, axis=1)
    meta = jnp.concatenate(
        [rounds[:, :, None], jnp.transpose(first, (0, 2, 1)), jnp.zeros((b, nblk, 31 - ne), I32)], axis=2)
    pos_t = jnp.transpose(pos, (0, 2, 1))
    y = _expert_ffn(idx_flat, hpk.reshape(b * n, d + AFF_W), w_gate, w_up, w_down, layer, b * cap)
    return _combine(meta.reshape(-1), y, pos_t, xl, g2, final_gain)


def _odd_pool_in_kernel(a_ref, w_ref, zp_ref):
    zp_ref[0] = jnp.dot(a_ref[0], w_ref[...], preferred_element_type=F32)


def _odd_fourier_in_kernel(a_ref, w_ref, cs_ref, uv_ref, *, groups):
    acc = jnp.dot(a_ref[0], w_ref[...], preferred_element_type=F32)
    gw = acc.shape[1] // groups
    for g in range(groups):
        zg = acc[:, g * gw:(g + 1) * gw].astype(BF)
        uv_ref[0, :, g * 2 * gw:(g + 1) * 2 * gw] = jnp.dot(
            zg, cs_ref[...], preferred_element_type=F32).astype(BF)


def _odd_in(h, w_pool, w_fourier, cs_c, groups):
    b, n, d = h.shape
    half = w_pool.shape[1]
    tm = _tile(n, 1024)
    a_spec = pl.BlockSpec((1, tm, d), lambda bi, i: (bi, i, 0))
    zp = pl.pallas_call(
        _odd_pool_in_kernel,
        grid=(b, n // tm),
        in_specs=[a_spec, _const_spec((d, half))],
        out_specs=pl.BlockSpec((1, tm, half), lambda bi, i: (bi, i, 0)),
        out_shape=jax.ShapeDtypeStruct((b, n, half), F32),
        compiler_params=_cparams("parallel", "parallel"),
        name="odd_pool_in",
    )(h, w_pool)
    uv = pl.pallas_call(
        functools.partial(_odd_fourier_in_kernel, groups=groups),
        grid=(b, n // tm),
        in_specs=[a_spec, _const_spec((d, half)), _const_spec(cs_c.shape)],
        out_specs=pl.BlockSpec((1, tm, 2 * half), lambda bi, i: (bi, i, 0)),
        out_shape=jax.ShapeDtypeStruct((b, n, 2 * half), BF),
        compiler_params=_cparams("parallel", "parallel"),
        name="odd_fourier_in",
    )(h, w_fourier, cs_c)
    return zp, uv


def _pool_kernel(prev_ref, cur_ref, next_ref, w_ref, s_ref, o_ref, *, n):
    i = pl.program_id(1)
    tm = cur_ref.shape[1]
    halo = prev_ref.shape[1]
    cur = cur_ref[0]
    prev = jnp.where(i > 0, prev_ref[0], 0.0)
    nxt = jnp.where(i < pl.num_programs(1) - 1, next_ref[0], 0.0)
    ext = jnp.concatenate([prev, cur, nxt], axis=0)
    t = i * tm + lax.broadcasted_iota(I32, (tm, 1), 0)
    gw = cur.shape[1] // len(POOL_WINDOWS)
    for g, win in enumerate(POOL_WINDOWS):
        hw = win // 2
        cols = slice(g * gw, (g + 1) * gw)
        tot = jnp.zeros((tm, gw), F32)
        for j in range(-hw, hw):
            tot = tot + ext[halo + j:halo + j + tm, cols]
        cnt = (jnp.minimum(t + hw, n) - jnp.maximum(t - hw, 0)).astype(F32)
        pooled = (tot / cnt - cur[:, cols]).astype(BF)
        o_ref[0, :, cols] = (jnp.dot(pooled, w_ref[g], preferred_element_type=F32) * s_ref[:, cols]).astype(BF)


def _pool_mix(zp, pool_w, pool_scale):
    b, n, width = zp.shape
    tm = _tile(n, 512)
    halo = 8
    assert max(POOL_WINDOWS) // 2 <= halo
    r = tm // halo
    nb = n // tm
    return pl.pallas_call(
        functools.partial(_pool_kernel, n=n),
        grid=(b, nb),
        in_specs=[
            pl.BlockSpec((1, halo, width), lambda bi, i: (bi, jnp.maximum(i * r - 1, 0), 0)),
            pl.BlockSpec((1, tm, width), lambda bi, i: (bi, i, 0)),
            pl.BlockSpec((1, halo, width), lambda bi, i: (bi, jnp.minimum((i + 1) * r, n // halo - 1), 0)),
            _const_spec(pool_w.shape),
            _const_spec((1, width)),
        ],
        out_specs=pl.BlockSpec((1, tm, width), lambda bi, i: (bi, i, 0)),
        out_shape=jax.ShapeDtypeStruct((b, n, width), BF),
        compiler_params=_cparams("parallel", "parallel"),
        name="pool_mix",
    )(zp, zp, zp, pool_w, pool_scale)


def _fourier_kernel(cn_ref, sn_ref, u_ref, v_ref, w_ref, o_ref):
    spec = (jnp.dot(cn_ref[...], u_ref[0], preferred_element_type=F32)
            - jnp.dot(sn_ref[...], v_ref[0], preferred_element_type=F32))
    o_ref[0] = jnp.dot(spec.astype(BF), w_ref[0], preferred_element_type=F32).astype(BF)


def _fourier_mix(uv, cn, sn, fw):
    b, n, w2 = uv.shape
    groups, gw, _ = fw.shape
    tj = _tile(n, 1024)
    return pl.pallas_call(
        _fourier_kernel,
        grid=(n // tj, b, groups),
        in_specs=[
            pl.BlockSpec((tj, n), lambda j, bi, g: (j, 0)),
            pl.BlockSpec((tj, n), lambda j, bi, g: (j, 0)),
            pl.BlockSpec((1, n, gw), lambda j, bi, g: (bi, 0, 2 * g)),
            pl.BlockSpec((1, n, gw), lambda j, bi, g: (bi, 0, 2 * g + 1)),
            pl.BlockSpec((1, gw, gw), lambda j, bi, g: (g, 0, 0)),
        ],
        out_specs=pl.BlockSpec((1, tj, gw), lambda j, bi, g: (bi, j, g)),
        out_shape=jax.ShapeDtypeStruct((b, n, groups * gw), BF),
        compiler_params=_cparams("parallel", "parallel", "parallel"),
        name="fourier_mix",
    )(cn, sn, uv, uv, fw)


def _dft_tables(n):
    r = 64 if n % 64 == 0 else 1
    k = jnp.arange(n, dtype=I32)

    def tables(j):
        ang = ((j[:, None] * k[None, :]) % n).astype(F32) * (2.0 * math.pi / n)
        return jnp.cos(ang), jnp.sin(ang)

    ca, sa = tables(jnp.arange(n // r, dtype=I32) * r)
    cb, sb = tables(jnp.arange(r, dtype=I32))
    s = 1.0 / math.sqrt(n)
    cos = (ca[:, None, :] * cb[None, :, :] - sa[:, None, :] * sb[None, :, :]) * s
    sin = (sa[:, None, :] * cb[None, :, :] + ca[:, None, :] * sb[None, :, :]) * s
    return cos.reshape(n, n), sin.reshape(n, n)


def kernel(x, c, ctx, c_ctx, mod_w, mod_b, norm_mix, norm_ffn, even_w_in, mla_q_norm, mla_w_uq, mla_kv_norm, mla_w_ukv, diff_lambda_q1, diff_lambda_k1, diff_lambda_q2, diff_lambda_k2, diff_subln, even_w_out, odd_w_in, pool_w, pool_scale, fourier_w, odd_w_out, router_w, expert_w_gate, expert_w_up, expert_w_down, final_norm):
    b, n, d = x.shape
    n_ctx = ctx.shape[1]

    c8 = jnp.concatenate([c, c_ctx[None, :], jnp.zeros((8 - b - 1, d), F32)], axis=0)
    mod = _modulation(c8, mod_w, mod_b)

    def mod_rows(i, k, ctx_row=False):
        sl = mod[i, :, k * d:(k + 1) * d]
        return sl[b:b + 1] if ctx_row else sl[:b, None, :]

    def router_mats(i):
        rw = router_w[i]
        rw_pad = jnp.concatenate([rw, jnp.zeros((d, LANE - N_EXPERTS), F32)], axis=1).astype(BF)
        return rw_pad, rw.T.astype(BF)

    i = 0
    w_in = even_w_in[0]
    ql, kvl, rp = MLA_Q_LORA, MLA_KV_LORA, MLA_ROPE
    perm_a = _rope_perm(MLA_ROPE // 4, 64 - MLA_ROPE // 2)
    perm_b = _rope_perm(DIFF_DIM // 4, 0)
    w_lora = jnp.concatenate(
        [w_in[:, :ql + kvl], _take_cols(w_in[:, ql + kvl:ql + kvl + rp], perm_a)], axis=1).astype(BF)
    c0 = ql + kvl + rp
    w_qk = w_in[:, c0:c0 + 2 * DIFF_QK].reshape(d, 2 * DIFF_QK // DIFF_DIM, DIFF_DIM)
    w_qk = jnp.take(w_qk, jnp.array(perm_b, I32), axis=2).reshape(d, 2 * DIFF_QK)
    w_diff = jnp.concatenate([w_qk, w_in[:, c0 + 2 * DIFF_QK:]], axis=1).astype(BF)
    w_uq = mla_w_uq[0].reshape(ql, MLA_HEADS, MLA_NOPE + MLA_ROPE)
    w_uq = jnp.concatenate(
        [w_uq[:, :, :MLA_NOPE],
         jnp.stack([_take_cols(w_uq[:, hd, MLA_NOPE:], perm_a) for hd in range(MLA_HEADS)], axis=1)],
        axis=2).reshape(ql, MLA_HEADS * 2 * LANE).astype(BF)
    w_ukv = mla_w_ukv[0].astype(BF)
    cs_a, sn_a = _rope_tables(n, n_ctx, MLA_ROPE // 4, 64 - MLA_ROPE // 2)
    cs_b, sn_b = _rope_tables(n, n_ctx, DIFF_DIM // 4, 0)

    h_all = _normmod_cat(x, ctx, norm_mix[i][None, :], mod_rows(i, 1), mod_rows(i, 0),
                         mod_rows(i, 1, True), mod_rows(i, 0, True))
    cq, ckv, kr = _proj_lora(h_all, w_lora, mla_q_norm[0][None, :], mla_kv_norm[0][None, :], cs_a, sn_a)
    qkv_d = _proj_diff(h_all, w_diff, cs_b, sn_b)
    q_a = _up_q(cq, w_uq, cs_a, sn_a, n)
    k_a, v_a = _up_kv(ckv, w_ukv, kr)
    o_a = _mla_attention(q_a, k_a, v_a, n)
    o_d = _diff_attention(qkv_d, diff_lambda_q1[0][None, :], diff_lambda_k1[0][None, :],
                          diff_lambda_q2[0][None, :], diff_lambda_k2[0][None, :],
                          diff_subln[0][None, :], n, i)
    rw_pad, rw_t = router_mats(i)
    xl, hpk, afft = _mixer_out(o_a, o_d, even_w_out[0].astype(BF), x, mod_rows(i, 2), norm_ffn[i][None, :],
                               mod_rows(i, 4), mod_rows(i, 3), rw_pad, rw_t)
    xl = _moe(hpk, afft, xl, mod_rows(i, 5), expert_w_gate, expert_w_up, expert_w_down, i, None)

    i = 1
    groups = fourier_w.shape[1]
    gw = fourier_w.shape[2]
    cc, sc_ = _dft_tables(gw)
    cs_c = jnp.concatenate([cc, sc_], axis=1).astype(BF)
    cn, sn = _dft_tables(n)
    h = _normmod(xl, norm_mix[i][None, :], mod_rows(i, 1), mod_rows(i, 0))
    pool_width = pool_w.shape[1] * pool_w.shape[2]
    w_odd = odd_w_in[0].astype(BF)
    zp, uv = _odd_in(h, w_odd[:, :pool_width], w_odd[:, pool_width:], cs_c, groups)
    yp = _pool_mix(zp, pool_w[0].astype(BF), pool_scale[0][None, :])
    yf = _fourier_mix(uv, cn.astype(BF), sn.astype(BF), fourier_w[0].astype(BF))
    rw_pad, rw_t = router_mats(i)
    xl, hpk, afft = _mixer_out(yp, yf, odd_w_out[0].astype(BF), xl, mod_rows(i, 2), norm_ffn[i][None, :],
                               mod_rows(i, 4), mod_rows(i, 3), rw_pad, rw_t)
    return _moe(hpk, afft, xl, mod_rows(i, 5), expert_w_gate, expert_w_up, expert_w_down, i,
                final_norm[None, :])
```
